```python
import jax, jax.numpy as jnp
from jax import lax
import numpy as np

D_MODEL = 1024
BATCH = 4
SEQ = 8192
DEPTH = 2
DEC_BATCH = 32
DEC_SEQ = 32
PAST_LEN = 2048

CHUNK = 64
Q_BLOCK = 128
D_FF = 2816
GLA_H = 4
GLA_DK = 64
GLA_DV = 128
GLA_RANK = 16
GLA_TAU = 16.0
DSA_H = 8
DSA_DH = 64
IDX_H = 8
IDX_D = 64
TOPK_MAX = 256
N_MOD = 9
LN_EPS = 1e-5
DEEPNORM_ALPHA = (2 * DEPTH) ** 0.25
DEEPNORM_BETA = (8 * DEPTH) ** -0.25
GLA_QK = GLA_H * GLA_DK
GLA_V = GLA_H * GLA_DV
DSA_W = DSA_H * DSA_DH
IDX_Q = IDX_H * IDX_D
IN_SPLITS = (GLA_QK, GLA_QK, GLA_V, GLA_V, GLA_RANK, DSA_W, DSA_W, DSA_W, IDX_Q, IDX_D, IDX_H, D_MODEL, D_MODEL)
D_IN = 2 * GLA_QK + 2 * GLA_V + GLA_RANK + 3 * DSA_W + IDX_Q + IDX_D + IDX_H + 2 * D_MODEL

kernel_name = 'hybrid_gla_dsa_streaming_encoder_step'


def _split_points():
    return [int(p) for p in np.cumsum(IN_SPLITS)[:-1]]


def _layer_norm(x, g, b):
    xf = x.astype(jnp.float32)
    mu = jnp.mean(xf, axis=-1, keepdims=True)
    var = jnp.mean(jnp.square(xf - mu), axis=-1, keepdims=True)
    return ((xf - mu) * lax.rsqrt(var + LN_EPS)).astype(x.dtype) * g + b


def _rms_norm(x, g):
    xf = x.astype(jnp.float32)
    return (xf * lax.rsqrt(jnp.mean(jnp.square(xf), axis=-1, keepdims=True) + LN_EPS)).astype(x.dtype) * g


def _modulate(x, shift, scale):
    return x * (1 + scale[:, None, :]) + shift[:, None, :]


def _swiglu(h, wg, wu, wd):
    return (jax.nn.silu(h @ wg) * (h @ wu)) @ wd


def _alibi_slopes(dtype):
    return jnp.asarray(2.0 ** (-8.0 * np.arange(1, DSA_H + 1, dtype=np.float32) / DSA_H), dtype)


def _gla_chunk(S, q, k, v, la):
    C = q.shape[1]
    b = lax.cumsum(la, axis=1)
    causal = jnp.tril(jnp.ones((C, C), dtype=bool))
    diff = b[:, :, None] - b[:, None, :]
    decay = jnp.exp(jnp.where(causal[None, :, :, None, None], diff, -jnp.inf))
    scores = jnp.einsum('bthk,btshk,bshk->bhts', q, decay, k)
    o = jnp.einsum('bthk,bhkv->bthv', q * jnp.exp(b), S) + jnp.einsum('bhts,bshv->bthv', scores, v)
    b_last = b[:, -1]
    S_new = jnp.exp(b_last)[..., None] * S + jnp.einsum('bshk,bshv->bhkv', k * jnp.exp(b_last[:, None] - b), v)
    return S_new, o


def _gla_scan(q, k, v, la, S0):
    B, T = q.shape[:2]
    n_chunks = T // CHUNK

    def to_chunks(a):
        return a.reshape(B, n_chunks, CHUNK, *a.shape[2:]).swapaxes(0, 1)

    S, o = lax.scan(lambda s, xs: _gla_chunk(s, *xs), S0, (to_chunks(q), to_chunks(k), to_chunks(v), to_chunks(la)))
    return o.swapaxes(0, 1).reshape(B, T, GLA_H, GLA_DV), S


def _dsa_attend(q, qi, w, q_pos, k_all, v_all, ki_all, k_pos, top_k):
    admissible = (k_pos[None, :] // CHUNK) <= (q_pos[:, None] // CHUNK)
    idx_logits = jnp.einsum('bqhd,bld->bqlh', qi, ki_all) * (IDX_D ** -0.5)
    index_score = jnp.einsum('bqlh,bqh->bql', jax.nn.relu(idx_logits), w).astype(jnp.float32)
    index_score = jnp.where(admissible[None], index_score, -jnp.inf)
    _, sel = lax.top_k(index_score, top_k)
    sel_pos = k_pos[sel]
    sel_ok = (sel_pos // CHUNK) <= (q_pos[None, :, None] // CHUNK)
    gather = jax.vmap(lambda rows, ids: rows[ids])
    k_sel = gather(k_all, sel)
    v_sel = gather(v_all, sel)
    dist = jnp.abs(q_pos[None, :, None] - sel_pos).astype(jnp.float32)
    slopes = _alibi_slopes(jnp.float32)
    logits = jnp.einsum('bqhd,bqkhd->bhqk', q, k_sel).astype(jnp.float32) * (DSA_DH ** -0.5)
    logits = logits - slopes[None, :, None, None] * dist[:, None]
    logits = jnp.where(sel_ok[:, None], logits, -jnp.inf)
    p = jax.nn.softmax(logits, axis=-1).astype(v_all.dtype)
    return jnp.einsum('bhqk,bqkhd->bqhd', p, v_sel)


def _dsa_prompt(q, qi, w, k, v, ki):
    B, T = q.shape[:2]
    n_blocks = T // Q_BLOCK
    top_k = min(TOPK_MAX, T // 4)
    pos = jnp.arange(T, dtype=jnp.int32)

    def blocks(a):
        return a.reshape(B, n_blocks, Q_BLOCK, *a.shape[2:]).swapaxes(0, 1)

    def body(xs):
        qb, qib, wb, pb = xs
        return _dsa_attend(qb, qib, wb, pb, k, v, ki, pos, top_k)

    o = lax.map(body, (blocks(q), blocks(qi), blocks(w), pos.reshape(n_blocks, Q_BLOCK)))
    return o.swapaxes(0, 1).reshape(B, T, DSA_H, DSA_DH)


def _token_mixer(h, w_in, w_alpha, b_alpha, gla_norm_g, w_br_gla, w_br_dsa, w_out, past):
    B, T, _ = h.shape
    (gq, gk, gv, gg, glr, dq, dk, dv, iq, ik, iw, ga, gb) = jnp.split(h @ w_in, _split_points(), axis=-1)
    gq = gq.reshape(B, T, GLA_H, GLA_DK) * (GLA_DK ** -0.5)
    gk = gk.reshape(B, T, GLA_H, GLA_DK)
    gv = gv.reshape(B, T, GLA_H, GLA_DV)
    la = (jax.nn.log_sigmoid(glr @ w_alpha + b_alpha) / GLA_TAU).reshape(B, T, GLA_H, GLA_DK)
    dq = dq.reshape(B, T, DSA_H, DSA_DH)
    dk = dk.reshape(B, T, DSA_H, DSA_DH)
    dv = dv.reshape(B, T, DSA_H, DSA_DH)
    iq = iq.reshape(B, T, IDX_H, IDX_D)
    iw = iw * (IDX_H ** -0.5)
    if past is None:
        S0 = jnp.zeros((B, GLA_H, GLA_DK, GLA_DV), h.dtype)
        o_gla, S = _gla_scan(gq, gk, gv, la, S0)
        o_dsa = _dsa_prompt(dq, iq, iw, dk, dv, ik)
    else:
        ck, cv, cki, S0 = past
        S, o_gla = _gla_chunk(S0, gq, gk, gv, la)
        k_all = jnp.concatenate([ck, dk], axis=1)
        v_all = jnp.concatenate([cv, dv], axis=1)
        ki_all = jnp.concatenate([cki, ik], axis=1)
        P = ck.shape[1]
        L = k_all.shape[1]
        q_pos = P + jnp.arange(T, dtype=jnp.int32)
        k_pos = jnp.arange(L, dtype=jnp.int32)
        o_dsa = _dsa_attend(dq, iq, iw, q_pos, k_all, v_all, ki_all, k_pos, min(TOPK_MAX, L // 4))
    o_gla = _rms_norm(o_gla, gla_norm_g) * jax.nn.silu(gg.reshape(B, T, GLA_H, GLA_DV))
    merged = (jax.nn.sigmoid(ga) * (o_gla.reshape(B, T, GLA_V) @ w_br_gla)
              + jax.nn.sigmoid(gb) * (o_dsa.reshape(B, T, DSA_W) @ w_br_dsa))
    return merged @ w_out, (dk, dv, ik, S)


def _encoder_layer(x, c, lp, past):
    (w_ada, b_ada, ln_g, ln_b, wg, wu, wd, w_in, w_alpha, b_alpha, gla_g, w_br_gla, w_br_dsa, w_out) = lp
    mod = c @ w_ada + b_ada
    sh1, sc1, g1, sh2, sc2, g2, sh3, sc3, g3 = jnp.split(mod, N_MOD, axis=-1)
    f1 = _swiglu(_modulate(x, sh1, sc1), wg[0], wu[0], wd[0])
    x = _layer_norm(DEEPNORM_ALPHA * x + 0.5 * g1[:, None] * f1, ln_g[0], ln_b[0])
    y, st = _token_mixer(_modulate(x, sh2, sc2), w_in, w_alpha, b_alpha, gla_g, w_br_gla, w_br_dsa, w_out, past)
    x = _layer_norm(DEEPNORM_ALPHA * x + g2[:, None] * y, ln_g[1], ln_b[1])
    f2 = _swiglu(_modulate(x, sh3, sc3), wg[1], wu[1], wd[1])
    x = _layer_norm(DEEPNORM_ALPHA * x + 0.5 * g3[:, None] * f2, ln_g[2], ln_b[2])
    return x, st


def setup_inputs(seed: int = 0) -> dict:
    key = jax.random.key(seed)
    ks = jax.random.split(key, 24)
    f32 = jnp.float32
    nrm = lambda k, shape, s: jax.random.normal(k, shape, f32) * s
    return {
        'x_prompt': nrm(ks[0], (BATCH, SEQ, D_MODEL), 1.0),
        'x_sample': nrm(ks[1], (DEC_BATCH, DEC_SEQ, D_MODEL), 1.0),
        'cache_dsa_k': nrm(ks[2], (DEPTH, DEC_BATCH, PAST_LEN, DSA_H, DSA_DH), 1.0),
        'cache_dsa_v': nrm(ks[3], (DEPTH, DEC_BATCH, PAST_LEN, DSA_H, DSA_DH), 1.0),
        'cache_idx_k': nrm(ks[4], (DEPTH, DEC_BATCH, PAST_LEN, IDX_D), 1.0),
        'state_gla': nrm(ks[5], (DEPTH, DEC_BATCH, GLA_H, GLA_DK, GLA_DV), 0.5),
        'c_prompt': nrm(ks[6], (BATCH, D_MODEL), 1.0),
        'c_sample': nrm(ks[7], (DEC_BATCH, D_MODEL), 1.0),
        'w_ada': nrm(ks[8], (DEPTH, D_MODEL, N_MOD * D_MODEL), 0.5 * D_MODEL ** -0.5),
        'b_ada': nrm(ks[9], (DEPTH, N_MOD * D_MODEL), 0.01),
        'ln_g': 1.0 + nrm(ks[10], (DEPTH, 3, D_MODEL), 0.02),
        'ln_b': nrm(ks[11], (DEPTH, 3, D_MODEL), 0.02),
        'ffn_w_gate': nrm(ks[12], (DEPTH, 2, D_MODEL, D_FF), D_MODEL ** -0.5),
        'ffn_w_up': nrm(ks[13], (DEPTH, 2, D_MODEL, D_FF), D_MODEL ** -0.5),
        'ffn_w_down': nrm(ks[14], (DEPTH, 2, D_FF, D_MODEL), DEEPNORM_BETA * D_FF ** -0.5),
        'w_in': nrm(ks[15], (DEPTH, D_MODEL, D_IN), D_MODEL ** -0.5),
        'w_alpha': nrm(ks[16], (DEPTH, GLA_RANK, GLA_QK), GLA_RANK ** -0.5),
        'b_alpha': nrm(ks[17], (DEPTH, GLA_QK), 0.1),
        'gla_norm_g': 1.0 + nrm(ks[18], (DEPTH, GLA_DV), 0.02),
        'w_br_gla': nrm(ks[19], (DEPTH, GLA_V, D_MODEL), GLA_V ** -0.5),
        'w_br_dsa': nrm(ks[20], (DEPTH, DSA_W, D_MODEL), DSA_W ** -0.5),
        'w_out': nrm(ks[21], (DEPTH, D_MODEL, D_MODEL), DEEPNORM_BETA * D_MODEL ** -0.5),
    }


def reference(x_prompt, x_sample, cache_dsa_k, cache_dsa_v, cache_idx_k, state_gla, c_prompt, c_sample,
              w_ada, b_ada, ln_g, ln_b, ffn_w_gate, ffn_w_up, ffn_w_down, w_in, w_alpha, b_alpha,
              gla_norm_g, w_br_gla, w_br_dsa, w_out):
    xp = x_prompt
    xs = x_sample
    kp, vp, kip, sp = [], [], [], []
    ksm, vsm, kism, ssm = [], [], [], []
    for l in range(DEPTH):
        lp = (w_ada[l], b_ada[l], ln_g[l], ln_b[l], ffn_w_gate[l], ffn_w_up[l], ffn_w_down[l],
              w_in[l], w_alpha[l], b_alpha[l], gla_norm_g[l], w_br_gla[l], w_br_dsa[l], w_out[l])
        xp, (k1, v1, ki1, s1) = _encoder_layer(xp, c_prompt, lp, None)
        past = (cache_dsa_k[l], cache_dsa_v[l], cache_idx_k[l], state_gla[l])
        xs, (k2, v2, ki2, s2) = _encoder_layer(xs, c_sample, lp, past)
        kp.append(k1); vp.append(v1); kip.append(ki1); sp.append(s1)
        ksm.append(k2); vsm.append(v2); kism.append(ki2); ssm.append(s2)
    new_k_prompt = jnp.stack(kp)
    new_v_prompt = jnp.stack(vp)
    new_idxk_prompt = jnp.stack(kip)
    new_gla_prompt = jnp.stack(sp)
    new_k_sample = jnp.stack(ksm)
    new_v_sample = jnp.stack(vsm)
    new_idxk_sample = jnp.stack(kism)
    new_gla_sample = jnp.stack(ssm)
    return (xp, xs, new_k_prompt, new_v_prompt, new_idxk_prompt, new_gla_prompt,
            new_k_sample, new_v_sample, new_idxk_sample, new_gla_sample)
```

```python
import functools

import jax
import jax.numpy as jnp
from jax import lax
from jax.experimental import pallas as pl
from jax.experimental.pallas import tpu as pltpu

CHUNK = 64
Q_BLOCK = 128
GLA_H = 4
GLA_DK = 64
GLA_DV = 128
GLA_RANK = 16
GLA_TAU = 16.0
DSA_H = 8
DSA_DH = 64
IDX_H = 8
IDX_D = 64
TOPK_MAX = 256
N_MOD = 9
LN_EPS = 1e-5
GLA_QK = GLA_H * GLA_DK
GLA_V = GLA_H * GLA_DV
DSA_W = DSA_H * DSA_DH
IDX_Q = IDX_H * IDX_D

LANES = 128
VMEM_LIMIT_BYTES = 56 * 1024 * 1024
MASK_VALUE = -1e30

F32 = jnp.float32
BF16 = jnp.bfloat16


def _dot(a, b):
    return jnp.dot(a, b, preferred_element_type=F32)


def _params(*sem):
    return pltpu.CompilerParams(dimension_semantics=sem, vmem_limit_bytes=VMEM_LIMIT_BYTES)


def _layer_norm(y, g, b):
    mu = jnp.mean(y, axis=-1, keepdims=True)
    d = y - mu
    var = jnp.mean(d * d, axis=-1, keepdims=True)
    return d * lax.rsqrt(var + LN_EPS) * g + b


def _silu(x):
    return x * jax.nn.sigmoid(x)


def _row_tiles(B, T, rows=512):
    if T >= 256:
        return 1, (rows if T % rows == 0 else 256)
    bb = max(1, min(B, 256 // T))
    while B % bb:
        bb -= 1
    return bb, T


def _ada_kernel(c_ref, w_ref, b_ref, o_ref):
    o_ref[...] = _dot(c_ref[...].astype(BF16), w_ref[...]) + b_ref[...]


def _ada(c, w, b):
    Bc, D = c.shape
    N = w.shape[1]
    return pl.pallas_call(
        _ada_kernel,
        grid=(N // D,),
        in_specs=[pl.BlockSpec((Bc, D), lambda j: (0, 0)),
                  pl.BlockSpec((D, D), lambda j: (0, j)),
                  pl.BlockSpec((1, D), lambda j: (0, j))],
        out_specs=pl.BlockSpec((Bc, D), lambda j: (0, j)),
        out_shape=jax.ShapeDtypeStruct((Bc, N), F32),
        compiler_params=_params("arbitrary"),
        name="ada",
    )(c, w, b)


def _mod_spec(k, bb, D):
    return pl.BlockSpec((None, bb, 1, D), lambda i, j: (k, i, 0, 0))


def _const_spec(shape):
    return pl.BlockSpec(shape, lambda i, j: (0,) * len(shape), pipeline_mode=pl.Buffered(1))


def _ffn_kernel(x_ref, sh_ref, sc_ref, gt_ref, wg_ref, wu_ref, wd_ref, lg_ref, lb_ref, o_ref, *, alpha, fc):
    x = x_ref[...]
    bb, tt, D = x.shape
    h = (x * (1.0 + sc_ref[...]) + sh_ref[...]).reshape(bb * tt, D).astype(BF16)
    f = jnp.zeros((bb * tt, D), F32)
    for j in range(wg_ref.shape[1] // fc):
        g = _dot(h, wg_ref[:, j * fc:(j + 1) * fc])
        u = _dot(h, wu_ref[:, j * fc:(j + 1) * fc])
        f = f + _dot((_silu(g) * u).astype(BF16), wd_ref[j * fc:(j + 1) * fc, :])
    y = alpha * x + 0.5 * gt_ref[...] * f.reshape(bb, tt, D)
    o_ref[...] = _layer_norm(y, lg_ref[...], lb_ref[...])


def _ffn(x, mod, ks, wg, wu, wd, lg, lb, alpha):
    B, T, D = x.shape
    FF = wg.shape[1]
    bb, tt = _row_tiles(B, T)
    fc = next(c for c in (512, 256, 128) if FF % c == 0)
    xspec = pl.BlockSpec((bb, tt, D), lambda i, j: (i, j, 0))
    return pl.pallas_call(
        functools.partial(_ffn_kernel, alpha=alpha, fc=fc),
        grid=(B // bb, T // tt),
        in_specs=[xspec, _mod_spec(ks[0], bb, D), _mod_spec(ks[1], bb, D), _mod_spec(ks[2], bb, D),
                  _const_spec((D, FF)), _const_spec((D, FF)), _const_spec((FF, D)),
                  _const_spec((1, D)), _const_spec((1, D))],
        out_specs=xspec,
        out_shape=jax.ShapeDtypeStruct((B, T, D), F32),
        compiler_params=_params("parallel", "parallel"),
        name="ffn",
    )(x, mod, mod, mod, wg, wu, wd, lg, lb)


_IN_SEGS = (("gq", GLA_QK), ("gk", GLA_QK), ("gv", GLA_V), ("gg", GLA_V), ("glr", LANES),
            ("dq", DSA_W), ("dk", DSA_W), ("dv", DSA_W), ("iq", IDX_Q), ("ikw", LANES))


def _log_sigmoid(z):
    return jnp.minimum(z, 0.0) - jnp.log(1.0 + jnp.exp(-jnp.abs(z)))


def _in_kernel(x_ref, sh_ref, sc_ref, w_ref, wa_ref, ba_ref,
               gq_o, gk_o, gv_o, gg_o, la_o, dq_o, dk_o, dv_o, iq_o, ikw_o, ga_o, gb_o):
    x = x_ref[...]
    bb, tt, D = x.shape
    h = (x * (1.0 + sc_ref[...]) + sh_ref[...]).reshape(bb * tt, D).astype(BF16)
    outs = dict(gq=gq_o, gk=gk_o, gv=gv_o, gg=gg_o, dq=dq_o, dk=dk_o, dv=dv_o, iq=iq_o, ikw=ikw_o, ga=ga_o, gb=gb_o)
    off = 0
    for name, n in _IN_SEGS + (("ga", D), ("gb", D)):
        r = _dot(h, w_ref[:, off:off + n])
        off += n
        if name == "glr":
            z = _dot(r.astype(BF16), wa_ref[...]) + ba_ref[...]
            la_o[...] = _log_sigmoid(z) * (1.0 / GLA_TAU)
        elif name == "gq":
            gq_o[...] = r * (GLA_DK ** -0.5)
        else:
            outs[name][...] = r


def _prep_w_in(w_in, w_alpha, D):
    pts, o = {}, 0
    for name, n in (("gq", GLA_QK), ("gk", GLA_QK), ("gv", GLA_V), ("gg", GLA_V), ("glr", GLA_RANK),
                    ("dq", DSA_W), ("dk", DSA_W), ("dv", DSA_W), ("iq", IDX_Q), ("ik", IDX_D), ("iw", IDX_H),
                    ("ga", D), ("gb", D)):
        pts[name] = w_in[:, o:o + n]
        o += n
    zeros = lambda n: jnp.zeros((D, n), w_in.dtype)
    cols = [pts["gq"], pts["gk"], pts["gv"], pts["gg"], pts["glr"], zeros(LANES - GLA_RANK),
            pts["dq"], pts["dk"], pts["dv"], pts["iq"], pts["ik"], pts["iw"], zeros(LANES - IDX_D - IDX_H),
            pts["ga"], pts["gb"]]
    w = jnp.concatenate(cols, axis=1).astype(BF16)
    wa = jnp.concatenate([w_alpha, jnp.zeros((LANES - GLA_RANK, GLA_QK), w_alpha.dtype)], axis=0).astype(BF16)
    return w, wa


def _inproj(x, mod, w, wa, ba):
    B, T, D = x.shape
    M = B * T
    bb, tt = _row_tiles(B, T, rows=256)
    nj = T // tt
    widths = [GLA_QK, GLA_QK, GLA_V, GLA_V, GLA_QK, DSA_W, DSA_W, DSA_W, IDX_Q, LANES, D, D]
    ospec = lambda n: pl.BlockSpec((bb * tt, n), lambda i, j: (i * nj + j, 0))
    return pl.pallas_call(
        _in_kernel,
        grid=(B // bb, nj),
        in_specs=[pl.BlockSpec((bb, tt, D), lambda i, j: (i, j, 0)), _mod_spec(3, bb, D), _mod_spec(4, bb, D),
                  _const_spec(w.shape), _const_spec(wa.shape), _const_spec((1, GLA_QK))],
        out_specs=[ospec(n) for n in widths],
        out_shape=[jax.ShapeDtypeStruct((M, n), F32) for n in widths],
        compiler_params=_params("parallel", "parallel"),
        name="inproj",
    )(x, mod, mod, w, wa, ba)


def _split3(a):
    hi = a.astype(BF16)
    r = a - hi.astype(F32)
    mid = r.astype(BF16)
    lo = (r - mid.astype(F32)).astype(BF16)
    return hi, mid, lo


def _gla_kernel(q_ref, kT_ref, la_ref, laT_ref, v_ref, gg_ref, s0_ref, ng_ref, o_ref, so_ref, s_ref, *, C, GW):
    j = pl.program_id(1)

    @pl.when(j == 0)
    def _():
        s_ref[...] = s0_ref[...]

    TB = q_ref.shape[0]
    G = GW // C
    r_i = lax.broadcasted_iota(jnp.int32, (GW, GW), 0)
    c_i = lax.broadcasted_iota(jnp.int32, (GW, GW), 1)
    same = (r_i // C) == (c_i // C)
    tri_lo = jnp.where(same & (c_i <= r_i), 1.0, 0.0).astype(BF16)
    tri_up = jnp.where(same & (r_i <= c_i), 1.0, 0.0).astype(BF16)
    lane = lax.broadcasted_iota(jnp.int32, (1, GW), 1)
    feat_head = lax.broadcasted_iota(jnp.int32, (1, GLA_QK), 1) // GLA_DK
    t_loc = lax.broadcasted_iota(jnp.int32, (C, GW), 0)
    s_lane = lax.broadcasted_iota(jnp.int32, (C, GW), 1)

    def group(g, carry):
        r0 = pl.multiple_of(g * GW, GW)
        q = q_ref[pl.ds(r0, GW), :]
        la = la_ref[pl.ds(r0, GW), :]
        v = v_ref[pl.ds(r0, GW), :].astype(BF16)
        gg = gg_ref[pl.ds(r0, GW), :]
        kT = kT_ref[:, pl.ds(r0, GW)]
        laT = laT_ref[:, pl.ds(r0, GW)]
        b = sum(_dot(tri_lo, p) for p in _split3(la))
        bT = sum(_dot(p, tri_up) for p in _split3(laT))
        qe = q * jnp.exp(b)
        keT = (kT * jnp.exp(-bT)).astype(BF16)
        for c in range(G):
            in_c = (lane // C) == c
            b_lastT = bT[:, c * C + C - 1:c * C + C]
            kdT = (kT * jnp.exp(jnp.where(in_c, b_lastT - bT, -jnp.inf))).astype(BF16)
            qc = qe[c * C:(c + 1) * C, :]
            qstack = jnp.concatenate([jnp.where(feat_head == h, qc, 0.0) for h in range(GLA_H)], axis=0).astype(BF16)
            s_old = s_ref[...]
            o_inter = _dot(qstack, s_old.astype(BF16))
            scores = _dot(qstack, keT)
            causal = ((s_lane // C) == c) & ((s_lane - c * C) <= t_loc)
            upd = _dot(kdT, v)
            upd = jnp.concatenate([upd[h * GLA_DK:(h + 1) * GLA_DK, h * GLA_DV:(h + 1) * GLA_DV]
                                   for h in range(GLA_H)], axis=0)
            s_ref[...] = jnp.exp(b_lastT) * s_old + upd
            for h in range(GLA_H):
                sc_h = jnp.where(causal, scores[h * C:(h + 1) * C, :], 0.0).astype(BF16)
                o_h = o_inter[h * C:(h + 1) * C, :] + _dot(sc_h, v[:, h * GLA_DV:(h + 1) * GLA_DV])
                o_h = o_h * lax.rsqrt(jnp.mean(o_h * o_h, axis=-1, keepdims=True) + LN_EPS) * ng_ref[...]
                o_h = o_h * _silu(gg[c * C:(c + 1) * C, h * GLA_DV:(h + 1) * GLA_DV])
                o_ref[pl.ds(r0 + c * C, C), h * GLA_DV:(h + 1) * GLA_DV] = o_h
        return carry

    lax.fori_loop(0, TB // GW, group, 0)
    so_ref[...] = s_ref[...]


def _gla(q, kT, la, laT, v, gg, s0, ng, C):
    B, T, _ = q.shape
    GW = min(LANES, T)
    TB = min(512, T)
    tok = lambda n: pl.BlockSpec((None, TB, n), lambda b, j: (b, j, 0))
    feat = pl.BlockSpec((None, GLA_QK, TB), lambda b, j: (b, 0, j))
    st = pl.BlockSpec((None, GLA_QK, GLA_DV), lambda b, j: (b, 0, 0))
    return pl.pallas_call(
        functools.partial(_gla_kernel, C=C, GW=GW),
        grid=(B, T // TB),
        in_specs=[tok(GLA_QK), feat, tok(GLA_QK), feat, tok(GLA_V), tok(GLA_V), st, _const_spec((1, GLA_DV))],
        out_specs=[tok(GLA_V), st],
        out_shape=[jax.ShapeDtypeStruct((B, T, GLA_V), F32), jax.ShapeDtypeStruct((B, GLA_QK, GLA_DV), F32)],
        scratch_shapes=[pltpu.VMEM((GLA_QK, GLA_DV), F32)],
        compiler_params=_params("parallel", "arbitrary"),
        name="gla",
    )(q, kT, la, laT, v, gg, s0, ng)


def _key_to_float(k):
    return lax.bitcast_convert_type(jnp.where(k >= 0, k, k ^ jnp.int32(0x7FFFFFFF)), F32)


def _dsa_kernel(qd_ref, qi_ref, ikw_ref, kT_ref, v_ref, kiT_ref, o_ref, sc_ref, *, QB, TK, P, L, top_k):
    qb = pl.program_id(1)
    q0 = P + qb * QB
    n_cols = jnp.minimum(((q0 + QB - 1) // CHUNK + 1) * CHUNK, L)
    nt = (n_cols + TK - 1) // TK
    q_pos = q0 + lax.broadcasted_iota(jnp.int32, (QB, 1), 0)
    q_chunk = q_pos // CHUNK
    all_sel = jnp.minimum((q_chunk + 1) * CHUNK, L) <= top_k
    lane = lax.broadcasted_iota(jnp.int32, (1, TK), 1)

    qi = qi_ref[...]
    w = ikw_ref[...]
    qi_h = [qi[:, h * IDX_D:(h + 1) * IDX_D].astype(BF16) for h in range(IDX_H)]
    w_h = [w[:, IDX_D + h:IDX_D + h + 1] * (IDX_H ** -0.5) for h in range(IDX_H)]

    def admissible(c0):
        k_pos = c0 + lane
        return k_pos, ((k_pos // CHUNK) <= q_chunk) & (k_pos < L)

    def score_tile(t, carry):
        c0 = pl.multiple_of(t * TK, TK)
        ki = kiT_ref[:, pl.ds(c0, TK)]
        acc = jnp.zeros((QB, TK), F32)
        for h in range(IDX_H):
            acc = acc + jnp.maximum(_dot(qi_h[h], ki) * (IDX_D ** -0.5), 0.0) * w_h[h]
        _, adm = admissible(c0)
        sc_ref[:, pl.ds(c0, TK)] = jnp.where(adm, acc, -jnp.inf)
        return carry

    lax.fori_loop(0, nt, score_tile, 0)

    def count(pred):
        def body(t, a):
            return a + jnp.where(pred(sc_ref[:, pl.ds(pl.multiple_of(t * TK, TK), TK)]), 1.0, 0.0)
        return jnp.sum(lax.fori_loop(0, nt, body, jnp.zeros((QB, TK), F32)), axis=-1, keepdims=True)

    def search(i, tkey):
        cand = tkey + lax.shift_left(jnp.int32(1), 31 - i)
        cf = _key_to_float(cand)
        return jnp.where(count(lambda s: s >= cf) >= top_k, cand, tkey)

    tkey = lax.fori_loop(0, 32, search, jnp.full((QB, 1), -2 ** 31, jnp.int32))
    thr = _key_to_float(tkey)
    n_ties = top_k - count(lambda s: s > thr)

    r_i = lax.broadcasted_iota(jnp.int32, (TK, TK), 0)
    c_i = lax.broadcasted_iota(jnp.int32, (TK, TK), 1)
    prefix = jnp.where(r_i <= c_i, 1.0, 0.0).astype(BF16)
    qd = qd_ref[...]
    qd_h = [qd[:, h * DSA_DH:(h + 1) * DSA_DH].astype(BF16) for h in range(DSA_H)]
    low_half = lax.broadcasted_iota(jnp.int32, (1, 2 * DSA_DH), 1) < DSA_DH

    def attend(t, carry):
        m, l, acc, seen = carry
        c0 = pl.multiple_of(t * TK, TK)
        s_idx = sc_ref[:, pl.ds(c0, TK)]
        eq = s_idx == thr
        rank = seen + _dot(jnp.where(eq, 1.0, 0.0).astype(BF16), prefix)
        k_pos, adm = admissible(c0)
        sel = ((s_idx > thr) | (eq & (rank <= n_ties)) | all_sel) & adm
        seen = rank[:, TK - 1:TK]
        dist = jnp.abs(q_pos - k_pos).astype(F32)
        kt = kT_ref[:, pl.ds(c0, TK)]
        vt = v_ref[pl.ds(c0, TK), :]
        m_new, l_new, acc_new = [], [], []
        for hp in range(DSA_H // 2):
            corr, pv = [], []
            for h in (2 * hp, 2 * hp + 1):
                s = _dot(qd_h[h], kt[h * DSA_DH:(h + 1) * DSA_DH, :]) * (DSA_DH ** -0.5) - (2.0 ** -(h + 1)) * dist
                s = jnp.where(sel, s, MASK_VALUE)
                mh = jnp.maximum(m[h], jnp.max(s, axis=-1, keepdims=True))
                p = jnp.where(sel, jnp.exp(s - mh), 0.0)
                ch = jnp.exp(m[h] - mh)
                m_new.append(mh)
                l_new.append(l[h] * ch + jnp.sum(p, axis=-1, keepdims=True))
                corr.append(ch)
                pv.append(_dot(p.astype(BF16), vt[:, hp * 2 * DSA_DH:(hp + 1) * 2 * DSA_DH]))
            acc_new.append(acc[hp] * jnp.where(low_half, corr[0], corr[1]) + jnp.where(low_half, pv[0], pv[1]))
        return tuple(m_new), tuple(l_new), tuple(acc_new), seen

    init = (tuple(jnp.full((QB, 1), MASK_VALUE, F32) for _ in range(DSA_H)),
            tuple(jnp.zeros((QB, 1), F32) for _ in range(DSA_H)),
            tuple(jnp.zeros((QB, 2 * DSA_DH), F32) for _ in range(DSA_H // 2)),
            jnp.zeros((QB, 1), F32))
    _, l, acc, _ = lax.fori_loop(0, nt, attend, init)
    for hp in range(DSA_H // 2):
        o_ref[:, hp * 2 * DSA_DH:(hp + 1) * 2 * DSA_DH] = acc[hp] / jnp.where(low_half, l[2 * hp], l[2 * hp + 1])


def _dsa(qd, qi, ikw, kT, v, kiT, P, L, top_k):
    B, Tq, _ = qd.shape
    Lp = v.shape[1]
    QB = min(Q_BLOCK, Tq)
    TK = 256
    qspec = lambda n: pl.BlockSpec((None, QB, n), lambda b, i: (b, i, 0))
    kspec = lambda r, c: pl.BlockSpec((None, r, c), lambda b, i: (b, 0, 0), pipeline_mode=pl.Buffered(1))
    return pl.pallas_call(
        functools.partial(_dsa_kernel, QB=QB, TK=TK, P=P, L=L, top_k=top_k),
        grid=(B, Tq // QB),
        in_specs=[qspec(DSA_W), qspec(IDX_Q), qspec(LANES), kspec(DSA_W, Lp), kspec(Lp, DSA_W), kspec(IDX_D, Lp)],
        out_specs=qspec(DSA_W),
        out_shape=jax.ShapeDtypeStruct((B, Tq, DSA_W), F32),
        scratch_shapes=[pltpu.VMEM((QB, Lp), F32)],
        compiler_params=_params("parallel", "arbitrary"),
        name="dsa",
    )(qd, qi, ikw, kT, v, kiT)


def _merge_kernel(x_ref, og_ref, od_ref, ga_ref, gb_ref, gt_ref, wbg_ref, wbd_ref, wo_ref, lg_ref, lb_ref, o_ref,
                  *, alpha):
    x = x_ref[...]
    bb, tt, D = x.shape
    merged = (jax.nn.sigmoid(ga_ref[...]) * _dot(og_ref[...].astype(BF16), wbg_ref[...])
              + jax.nn.sigmoid(gb_ref[...]) * _dot(od_ref[...].astype(BF16), wbd_ref[...]))
    y = _dot(merged.astype(BF16), wo_ref[...]).reshape(bb, tt, D)
    o_ref[...] = _layer_norm(alpha * x + gt_ref[...] * y, lg_ref[...], lb_ref[...])


def _merge(x, og, od, ga, gb, mod, wbg, wbd, wo, lg, lb, alpha):
    B, T, D = x.shape
    bb, tt = _row_tiles(B, T)
    nj = T // tt
    xspec = pl.BlockSpec((bb, tt, D), lambda i, j: (i, j, 0))
    rspec = lambda n: pl.BlockSpec((bb * tt, n), lambda i, j: (i * nj + j, 0))
    return pl.pallas_call(
        functools.partial(_merge_kernel, alpha=alpha),
        grid=(B // bb, nj),
        in_specs=[xspec, rspec(GLA_V), rspec(DSA_W), rspec(D), rspec(D), _mod_spec(5, bb, D),
                  _const_spec(wbg.shape), _const_spec(wbd.shape), _const_spec(wo.shape),
                  _const_spec((1, D)), _const_spec((1, D))],
        out_specs=xspec,
        out_shape=jax.ShapeDtypeStruct((B, T, D), F32),
        compiler_params=_params("parallel", "parallel"),
        name="merge",
    )(x, og, od, ga, gb, mod, wbg, wbd, wo, lg, lb)


def _round_up(n, m):
    return (n + m - 1) // m * m


def _encoder_layer(x, mod, lp, past, alpha):
    B, T, D = x.shape
    x = _ffn(x, mod, (0, 1, 2), lp["wg"][0], lp["wu"][0], lp["wd"][0], lp["ln_g"][0:1], lp["ln_b"][0:1], alpha)
    gq, gk, gv, gg, la, dq, dk, dv, iq, ikw, ga, gb = _inproj(x, mod, lp["w_in"], lp["w_alpha"], lp["b_alpha"])
    b3 = lambda a: a.reshape(B, T, a.shape[-1])
    ik = b3(ikw)[:, :, :IDX_D]
    if past is None:
        s0 = jnp.zeros((B, GLA_QK, GLA_DV), F32)
        k_all, v_all, ki_all, P = b3(dk), b3(dv), ik, 0
        C = CHUNK
    else:
        ck, cv, cki, s_past = past
        P = ck.shape[1]
        s0 = s_past.reshape(B, GLA_QK, GLA_DV)
        k_all = jnp.concatenate([ck.reshape(B, P, DSA_W), b3(dk)], axis=1)
        v_all = jnp.concatenate([cv.reshape(B, P, DSA_W), b3(dv)], axis=1)
        ki_all = jnp.concatenate([cki, ik], axis=1)
        C = T
    og, s_new = _gla(b3(gq), b3(gk).swapaxes(1, 2), b3(la), b3(la).swapaxes(1, 2), b3(gv), b3(gg), s0,
                     lp["gla_g"], C)
    L = P + T
    Lp = _round_up(L, 256)
    pad = lambda a: jnp.pad(a.astype(BF16), ((0, 0), (0, Lp - L), (0, 0)))
    od = _dsa(b3(dq), b3(iq), b3(ikw), pad(k_all).swapaxes(1, 2), pad(v_all), pad(ki_all).swapaxes(1, 2),
              P, L, min(TOPK_MAX, L // 4))
    x = _merge(x, og.reshape(B * T, GLA_V), od.reshape(B * T, DSA_W), ga, gb, mod,
               lp["w_br_gla"], lp["w_br_dsa"], lp["w_out"], lp["ln_g"][1:2], lp["ln_b"][1:2], alpha)
    x = _ffn(x, mod, (6, 7, 8), lp["wg"][1], lp["wu"][1], lp["wd"][1], lp["ln_g"][2:3], lp["ln_b"][2:3], alpha)
    st = (dk.reshape(B, T, DSA_H, DSA_DH), dv.reshape(B, T, DSA_H, DSA_DH), ik,
          s_new.reshape(B, GLA_H, GLA_DK, GLA_DV))
    return x, st


def kernel(x_prompt, x_sample, cache_dsa_k, cache_dsa_v, cache_idx_k, state_gla, c_prompt, c_sample,
           w_ada, b_ada, ln_g, ln_b, ffn_w_gate, ffn_w_up, ffn_w_down, w_in, w_alpha, b_alpha,
           gla_norm_g, w_br_gla, w_br_dsa, w_out):
    depth = w_ada.shape[0]
    D = x_prompt.shape[-1]
    Bp, Bs = x_prompt.shape[0], x_sample.shape[0]
    alpha = (2 * depth) ** 0.25
    c_all = jnp.concatenate([c_prompt, c_sample], axis=0)
    xp, xs = x_prompt, x_sample
    outs = [[] for _ in range(8)]
    for l in range(depth):
        w_in_l, w_alpha_l = _prep_w_in(w_in[l], w_alpha[l], D)
        lp = dict(wg=ffn_w_gate[l].astype(BF16), wu=ffn_w_up[l].astype(BF16), wd=ffn_w_down[l].astype(BF16),
                  ln_g=ln_g[l], ln_b=ln_b[l], w_in=w_in_l, w_alpha=w_alpha_l, b_alpha=b_alpha[l][None, :],
                  gla_g=gla_norm_g[l][None, :], w_br_gla=w_br_gla[l].astype(BF16),
                  w_br_dsa=w_br_dsa[l].astype(BF16), w_out=w_out[l].astype(BF16))
        mod = _ada(c_all, w_ada[l].astype(BF16), b_ada[l][None, :])
        mod = mod.reshape(Bp + Bs, N_MOD, 1, D).swapaxes(0, 1)
        xp, st_p = _encoder_layer(xp, mod[:, :Bp], lp, None, alpha)
        past = (cache_dsa_k[l], cache_dsa_v[l], cache_idx_k[l], state_gla[l])
        xs, st_s = _encoder_layer(xs, mod[:, Bp:], lp, past, alpha)
        for i, a in enumerate(st_p + st_s):
            outs[i].append(a)
    return (xp, xs) + tuple(jnp.stack(o) for o in outs)
```

```python
import functools

import jax
import jax.numpy as jnp
from jax import lax
from jax.experimental import pallas as pl
from jax.experimental.pallas import tpu as pltpu

CHUNK = 64
Q_BLOCK = 128
GLA_H = 4
GLA_DK = 64
GLA_DV = 128
GLA_RANK = 16
GLA_TAU = 16.0
DSA_H = 8
DSA_DH = 64
IDX_H = 8
IDX_D = 64
TOPK_MAX = 256
N_MOD = 9
LN_EPS = 1e-5
GLA_QK = GLA_H * GLA_DK
GLA_V = GLA_H * GLA_DV
DSA_W = DSA_H * DSA_DH
IDX_Q = IDX_H * IDX_D

LANES = 128
VMEM_LIMIT_BYTES = 56 * 1024 * 1024
MASK_VALUE = -1e30
LOG2E = 1.4426950408889634
DSA_TK = 256
DSA_UNROLL = 2

F32 = jnp.float32
BF16 = jnp.bfloat16


def _dot(a, b):
    return jnp.dot(a, b, preferred_element_type=F32)


def _params(*sem):
    return pltpu.CompilerParams(dimension_semantics=sem, vmem_limit_bytes=VMEM_LIMIT_BYTES)


def _round_up(n, m):
    return (n + m - 1) // m * m


def _layer_norm(y, g, b):
    mu = jnp.mean(y, axis=-1, keepdims=True)
    d = y - mu
    var = jnp.mean(d * d, axis=-1, keepdims=True)
    return d * lax.rsqrt(var + LN_EPS) * g + b


def _silu(x):
    return x * jax.nn.sigmoid(x)


def _split3(a):
    hi = a.astype(BF16)
    r = a - hi.astype(F32)
    mid = r.astype(BF16)
    lo = (r - mid.astype(F32)).astype(BF16)
    return hi, mid, lo


def _row_tiles(B, T, rows=512):
    if T >= 256:
        return 1, (rows if T % rows == 0 else 256)
    bb = max(1, min(B, 256 // T))
    while B % bb:
        bb -= 1
    return bb, T


def _ada_kernel(c_ref, w_ref, b_ref, o_ref):
    o_ref[...] = _dot(c_ref[...].astype(BF16), w_ref[...]) + b_ref[...]


def _ada(c, w, b):
    Bc, D = c.shape
    N = w.shape[1]
    return pl.pallas_call(
        _ada_kernel,
        grid=(N // D,),
        in_specs=[pl.BlockSpec((Bc, D), lambda j: (0, 0)),
                  pl.BlockSpec((D, D), lambda j: (0, j)),
                  pl.BlockSpec((1, D), lambda j: (0, j))],
        out_specs=pl.BlockSpec((Bc, D), lambda j: (0, j)),
        out_shape=jax.ShapeDtypeStruct((Bc, N), F32),
        compiler_params=_params("arbitrary"),
        name="ada",
    )(c, w, b)


def _mod_spec(k, bb, D):
    return pl.BlockSpec((None, bb, 1, D), lambda i, j: (k, i, 0, 0))


def _const_spec(shape):
    return pl.BlockSpec(shape, lambda i, j: (0,) * len(shape), pipeline_mode=pl.Buffered(1))


def _ffn_kernel(x_ref, sh_ref, sc_ref, gt_ref, wg_ref, wu_ref, wd_ref, lg_ref, lb_ref, o_ref, *, alpha, fc):
    x = x_ref[...]
    bb, tt, D = x.shape
    h = (x * (1.0 + sc_ref[...]) + sh_ref[...]).reshape(bb * tt, D).astype(BF16)
    f = jnp.zeros((bb * tt, D), F32)
    for j in range(wg_ref.shape[1] // fc):
        g = _dot(h, wg_ref[:, j * fc:(j + 1) * fc])
        u = _dot(h, wu_ref[:, j * fc:(j + 1) * fc])
        f = f + _dot((_silu(g) * u).astype(BF16), wd_ref[j * fc:(j + 1) * fc, :])
    y = alpha * x + 0.5 * gt_ref[...] * f.reshape(bb, tt, D)
    o_ref[...] = _layer_norm(y, lg_ref[...], lb_ref[...])


def _ffn(x, mod, ks, wg, wu, wd, lg, lb, alpha):
    B, T, D = x.shape
    FF = wg.shape[1]
    bb, tt = _row_tiles(B, T)
    fc = next(c for c in (512, 256, 128) if FF % c == 0)
    xspec = pl.BlockSpec((bb, tt, D), lambda i, j: (i, j, 0))
    return pl.pallas_call(
        functools.partial(_ffn_kernel, alpha=alpha, fc=fc),
        grid=(B // bb, T // tt),
        in_specs=[xspec, _mod_spec(ks[0], bb, D), _mod_spec(ks[1], bb, D), _mod_spec(ks[2], bb, D),
                  _const_spec((D, FF)), _const_spec((D, FF)), _const_spec((FF, D)),
                  _const_spec((1, D)), _const_spec((1, D))],
        out_specs=xspec,
        out_shape=jax.ShapeDtypeStruct((B, T, D), F32),
        compiler_params=_params("parallel", "parallel"),
        name="ffn",
    )(x, mod, mod, mod, wg, wu, wd, lg, lb)


_IN_SEGS = (("gq", GLA_QK), ("gk", GLA_QK), ("gv", GLA_V), ("gg", GLA_V), ("glr", LANES),
            ("dq", DSA_W), ("dk", DSA_W), ("dv", DSA_W), ("iq", IDX_Q), ("ikw", LANES))


def _log_sigmoid(z):
    return jnp.minimum(z, 0.0) - jnp.log(1.0 + jnp.exp(-jnp.abs(z)))


def _in_kernel(x_ref, sh_ref, sc_ref, w_ref, wa_ref, ba_ref,
               gq_o, gk_o, gv_o, gg_o, la_o, dq_o, dk_o, dv_o, iq_o, ikw_o, ga_o, gb_o):
    x = x_ref[...]
    bb, tt, D = x.shape
    h = (x * (1.0 + sc_ref[...]) + sh_ref[...]).reshape(bb * tt, D).astype(BF16)
    outs = dict(gq=gq_o, gk=gk_o, gv=gv_o, gg=gg_o, dq=dq_o, dk=dk_o, dv=dv_o, iq=iq_o, ikw=ikw_o, ga=ga_o, gb=gb_o)
    off = 0
    for name, n in _IN_SEGS + (("ga", D), ("gb", D)):
        r = _dot(h, w_ref[:, off:off + n])
        off += n
        if name == "glr":
            z = _dot(r.astype(BF16), wa_ref[...]) + ba_ref[...]
            la_o[...] = _log_sigmoid(z) * (1.0 / GLA_TAU)
        elif name == "gq":
            gq_o[...] = r * (GLA_DK ** -0.5)
        else:
            outs[name][...] = r


def _prep_w_in(w_in, w_alpha, D):
    pts, o = {}, 0
    for name, n in (("gq", GLA_QK), ("gk", GLA_QK), ("gv", GLA_V), ("gg", GLA_V), ("glr", GLA_RANK),
                    ("dq", DSA_W), ("dk", DSA_W), ("dv", DSA_W), ("iq", IDX_Q), ("ik", IDX_D), ("iw", IDX_H),
                    ("ga", D), ("gb", D)):
        pts[name] = w_in[:, o:o + n]
        o += n
    zeros = lambda n: jnp.zeros((D, n), w_in.dtype)
    cols = [pts["gq"], pts["gk"], pts["gv"], pts["gg"], pts["glr"], zeros(LANES - GLA_RANK),
            pts["dq"], pts["dk"], pts["dv"], pts["iq"], pts["ik"], pts["iw"], zeros(LANES - IDX_D - IDX_H),
            pts["ga"], pts["gb"]]
    w = jnp.concatenate(cols, axis=1).astype(BF16)
    wa = jnp.concatenate([w_alpha, jnp.zeros((LANES - GLA_RANK, GLA_QK), w_alpha.dtype)], axis=0).astype(BF16)
    return w, wa


def _inproj(x, mod, w, wa, ba):
    B, T, D = x.shape
    M = B * T
    bb, tt = _row_tiles(B, T, rows=256)
    nj = T // tt
    widths = [GLA_QK, GLA_QK, GLA_V, GLA_V, GLA_QK, DSA_W, DSA_W, DSA_W, IDX_Q, LANES, D, D]
    ospec = lambda n: pl.BlockSpec((bb * tt, n), lambda i, j: (i * nj + j, 0))
    return pl.pallas_call(
        _in_kernel,
        grid=(B // bb, nj),
        in_specs=[pl.BlockSpec((bb, tt, D), lambda i, j: (i, j, 0)), _mod_spec(3, bb, D), _mod_spec(4, bb, D),
                  _const_spec(w.shape), _const_spec(wa.shape), _const_spec((1, GLA_QK))],
        out_specs=[ospec(n) for n in widths],
        out_shape=[jax.ShapeDtypeStruct((M, n), F32) for n in widths],
        compiler_params=_params("parallel", "parallel"),
        name="inproj",
    )(x, mod, mod, w, wa, ba)


def _gla_kernel(q_ref, kT_ref, la_ref, laT_ref, v_ref, gg_ref, s0_ref, ng_ref, o_ref, so_ref, s_ref, *, C, GW):
    j = pl.program_id(1)

    @pl.when(j == 0)
    def _():
        s_ref[...] = s0_ref[...]

    TB = q_ref.shape[0]
    G = GW // C
    r_i = lax.broadcasted_iota(jnp.int32, (GW, GW), 0)
    c_i = lax.broadcasted_iota(jnp.int32, (GW, GW), 1)
    same = (r_i // C) == (c_i // C)
    tri_lo = jnp.where(same & (c_i <= r_i), 1.0, 0.0).astype(BF16)
    tri_up = jnp.where(same & (r_i <= c_i), 1.0, 0.0).astype(BF16)
    lane = lax.broadcasted_iota(jnp.int32, (1, GW), 1)
    feat_head = lax.broadcasted_iota(jnp.int32, (1, GLA_QK), 1) // GLA_DK
    t_loc = lax.broadcasted_iota(jnp.int32, (C, GW), 0)
    s_lane = lax.broadcasted_iota(jnp.int32, (C, GW), 1)

    def group(g, carry):
        r0 = pl.multiple_of(g * GW, GW)
        q = q_ref[pl.ds(r0, GW), :]
        la = la_ref[pl.ds(r0, GW), :]
        v = v_ref[pl.ds(r0, GW), :].astype(BF16)
        gg = gg_ref[pl.ds(r0, GW), :]
        kT = kT_ref[:, pl.ds(r0, GW)]
        laT = laT_ref[:, pl.ds(r0, GW)]
        b = sum(_dot(tri_lo, p) for p in _split3(la))
        bT = sum(_dot(p, tri_up) for p in _split3(laT))
        qe = q * jnp.exp(b)
        keT = (kT * jnp.exp(-bT)).astype(BF16)
        for c in range(G):
            in_c = (lane // C) == c
            b_lastT = bT[:, c * C + C - 1:c * C + C]
            kdT = (kT * jnp.exp(jnp.where(in_c, b_lastT - bT, -jnp.inf))).astype(BF16)
            qc = qe[c * C:(c + 1) * C, :]
            qstack = jnp.concatenate([jnp.where(feat_head == h, qc, 0.0) for h in range(GLA_H)], axis=0).astype(BF16)
            s_old = s_ref[...]
            o_inter = _dot(qstack, s_old.astype(BF16))
            scores = _dot(qstack, keT)
            causal = ((s_lane // C) == c) & ((s_lane - c * C) <= t_loc)
            upd = _dot(kdT, v)
            upd = jnp.concatenate([upd[h * GLA_DK:(h + 1) * GLA_DK, h * GLA_DV:(h + 1) * GLA_DV]
                                   for h in range(GLA_H)], axis=0)
            s_ref[...] = jnp.exp(b_lastT) * s_old + upd
            for h in range(GLA_H):
                sc_h = jnp.where(causal, scores[h * C:(h + 1) * C, :], 0.0).astype(BF16)
                o_h = o_inter[h * C:(h + 1) * C, :] + _dot(sc_h, v[:, h * GLA_DV:(h + 1) * GLA_DV])
                o_h = o_h * lax.rsqrt(jnp.mean(o_h * o_h, axis=-1, keepdims=True) + LN_EPS) * ng_ref[...]
                o_h = o_h * _silu(gg[c * C:(c + 1) * C, h * GLA_DV:(h + 1) * GLA_DV])
                o_ref[pl.ds(r0 + c * C, C), h * GLA_DV:(h + 1) * GLA_DV] = o_h
        return carry

    lax.fori_loop(0, TB // GW, group, 0)
    so_ref[...] = s_ref[...]


def _gla(q, kT, la, laT, v, gg, s0, ng, C):
    B, T, _ = q.shape
    GW = min(LANES, T)
    TB = min(512, T)
    tok = lambda n: pl.BlockSpec((None, TB, n), lambda b, j: (b, j, 0))
    feat = pl.BlockSpec((None, GLA_QK, TB), lambda b, j: (b, 0, j))
    st = pl.BlockSpec((None, GLA_QK, GLA_DV), lambda b, j: (b, 0, 0))
    return pl.pallas_call(
        functools.partial(_gla_kernel, C=C, GW=GW),
        grid=(B, T // TB),
        in_specs=[tok(GLA_QK), feat, tok(GLA_QK), feat, tok(GLA_V), tok(GLA_V), st, _const_spec((1, GLA_DV))],
        out_specs=[tok(GLA_V), st],
        out_shape=[jax.ShapeDtypeStruct((B, T, GLA_V), F32), jax.ShapeDtypeStruct((B, GLA_QK, GLA_DV), F32)],
        scratch_shapes=[pltpu.VMEM((GLA_QK, GLA_DV), F32)],
        compiler_params=_params("parallel", "arbitrary"),
        name="gla",
    )(q, kT, la, laT, v, gg, s0, ng)


def _fold_rows(x, op):
    parts = [x[i:i + 8] for i in range(0, x.shape[0], 8)]
    while len(parts) > 1:
        parts = [op(parts[i], parts[i + 1]) for i in range(0, len(parts), 2)]
    return parts[0]


def _key_to_float(k):
    return lax.bitcast_convert_type(jnp.where(k >= 0, k, k ^ jnp.int32(0x7FFFFFFF)), F32)


def _dsa_kernel(qdT_ref, qiT_ref, wT_ref, posf_ref, k_ref, vT_ref, ki_ref, oT_ref, sc_ref, s_ref, *, P, L, top_k):
    QB, TK = LANES, DSA_TK
    qb = pl.program_id(1)
    q0 = P + qb * QB
    n_cols = jnp.minimum(((q0 + QB - 1) // CHUNK + 1) * CHUNK, L)
    STEP = DSA_UNROLL * TK
    nt = (n_cols + STEP - 1) // STEP
    t_diag = q0 // STEP
    q_pos = q0 + lax.broadcasted_iota(jnp.int32, (1, QB), 1)
    q_chunk = q_pos // CHUNK
    all_sel = jnp.minimum((q_chunk + 1) * CHUNK, L) <= top_k
    row = lax.broadcasted_iota(jnp.int32, (TK, 1), 0)

    def tile_start(t):
        return pl.multiple_of(t * TK, TK)

    qiT = qiT_ref[...].astype(BF16)
    qi_pair = [jnp.concatenate([qiT[h * IDX_D:(h + 1) * IDX_D], qiT[(h + 1) * IDX_D:(h + 2) * IDX_D]], axis=1)
               for h in range(0, IDX_H, 2)]
    wT = wT_ref[...] * (IDX_H ** -0.5 * IDX_D ** -0.5)

    def steps(lo, hi, tile_fn, carry, **kw):
        def body(t, c):
            for u in range(DSA_UNROLL):
                c = tile_fn(t * DSA_UNROLL + u, c, slot=u, **kw)
            return c
        return lax.fori_loop(lo, hi, body, carry)

    def score_tile(t, carry, slot, diag):
        c0 = tile_start(t)
        ki = ki_ref[pl.ds(c0, TK), :]
        acc = jnp.zeros((TK, QB), F32)
        for hp in range(IDX_H // 2):
            lg = _dot(ki, qi_pair[hp])
            for e in range(2):
                acc = acc + jnp.maximum(lg[:, e * QB:(e + 1) * QB], 0.0) * wT[2 * hp + e:2 * hp + e + 1, :]
        if diag:
            k_pos = c0 + row
            acc = jnp.where(((k_pos // CHUNK) <= q_chunk) & (k_pos < L), acc, -jnp.inf)
        sc_ref[pl.ds(c0, TK), :] = acc
        return carry

    steps(0, t_diag, score_tile, 0, diag=False)
    steps(t_diag, nt, score_tile, 0, diag=True)

    def count(pred):
        def body(t, a):
            hit = jnp.where(pred(sc_ref[pl.ds(pl.multiple_of(t * STEP, STEP), STEP), :]), 1.0, 0.0)
            return a + _fold_rows(hit, jnp.add)
        return jnp.sum(lax.fori_loop(0, nt, body, jnp.zeros((8, QB), F32)), axis=0, keepdims=True)

    def search(i, tkey):
        cand = tkey + lax.shift_left(jnp.int32(1), 31 - i)
        cf = _key_to_float(cand)
        return jnp.where(count(lambda s: s >= cf) >= top_k, cand, tkey)

    tkey = lax.fori_loop(0, 32, search, jnp.full((1, QB), -2 ** 31, jnp.int32))
    thr = _key_to_float(tkey)
    n_ties = top_k - count(lambda s: s > thr)
    thr_sel = jnp.where(all_sel, -3e38, thr)
    n_ties = jnp.where(all_sel, 3e38, n_ties)

    r_i = lax.broadcasted_iota(jnp.int32, (TK, TK), 0)
    c_i = lax.broadcasted_iota(jnp.int32, (TK, TK), 1)
    tril = jnp.where(c_i <= r_i, 1.0, 0.0).astype(BF16)
    qdT = qdT_ref[...] * (DSA_DH ** -0.5 * LOG2E)
    zeros = jnp.zeros((DSA_DH, QB), F32)
    feat_row = lax.broadcasted_iota(jnp.int32, (LANES, 1), 0)
    q_pos_f = q_pos.astype(F32)
    rhs = []
    for hp in range(DSA_H // 2):
        h0 = 2 * hp * DSA_DH
        qbd = jnp.concatenate([jnp.concatenate([qdT[h0:h0 + DSA_DH], zeros], axis=1),
                               jnp.concatenate([zeros, qdT[h0 + DSA_DH:h0 + 2 * DSA_DH]], axis=1)], axis=0)
        pos = []
        for h in (2 * hp, 2 * hp + 1):
            slope = (2.0 ** -(h + 1)) * LOG2E
            parts = _split3(-slope * q_pos_f) + _split3(jnp.full((1, QB), slope * CHUNK, F32)) \
                + _split3(jnp.full((1, QB), slope, F32))
            m = jnp.zeros((LANES, QB), F32)
            for r, part in enumerate(parts):
                m = jnp.where(feat_row == r, part.astype(F32), m)
            pos.append(m)
        rhs.append(jnp.concatenate([qbd, jnp.concatenate(pos, axis=1)], axis=0).astype(BF16))

    def attend(t, carry, slot, diag):
        stage = s_ref.at[slot]
        m, l, acc, seen = carry
        c0 = tile_start(t)
        s_idx = sc_ref[pl.ds(c0, TK), :]
        eq = s_idx == thr
        rank = seen + _dot(tril, jnp.where(eq, 1.0, 0.0).astype(BF16))
        seen = rank[TK - 1:TK, :]
        s_eff = jnp.where(jnp.where(eq, rank, 0.0) > n_ties, -jnp.inf, s_idx)
        bias = jnp.where(s_eff >= thr_sel, 0.0, MASK_VALUE)
        kt = k_ref[pl.ds(c0, TK), :]
        vt = vT_ref[:, pl.ds(c0, TK)]
        pf = posf_ref[pl.ds(c0, TK), :]
        if diag:
            over = jnp.maximum(c0 + row - q_pos, 0).astype(F32)
        for hp in range(DSA_H // 2):
            sp = _dot(jnp.concatenate([kt[:, hp * LANES:(hp + 1) * LANES], pf], axis=1), rhs[hp])
            for e in range(2):
                h = 2 * hp + e
                s = sp[:, e * QB:(e + 1) * QB] + bias
                if diag:
                    s = s - (2.0 * (2.0 ** -(h + 1)) * LOG2E) * over
                stage[h] = s
        m_new = [jnp.maximum(m[h], jnp.max(_fold_rows(stage[h], jnp.maximum), axis=0, keepdims=True))
                 for h in range(DSA_H)]
        l_new, acc_new = [], []
        for h in range(DSA_H):
            p = jnp.exp2(stage[h] - m_new[h])
            ch = jnp.exp2(m[h] - m_new[h])
            l_new.append(l[h] * ch + jnp.sum(_fold_rows(p, jnp.add), axis=0, keepdims=True))
            acc_new.append(acc[h] * ch + _dot(vt[h * DSA_DH:(h + 1) * DSA_DH, :], p.astype(BF16)))
        return tuple(m_new), tuple(l_new), tuple(acc_new), seen

    carry = (tuple(jnp.full((1, QB), MASK_VALUE, F32) for _ in range(DSA_H)),
             tuple(jnp.zeros((1, QB), F32) for _ in range(DSA_H)),
             tuple(jnp.zeros((DSA_DH, QB), F32) for _ in range(DSA_H)),
             jnp.zeros((1, QB), F32))
    carry = steps(0, t_diag, attend, carry, diag=False)
    _, l, acc, _ = steps(t_diag, nt, attend, carry, diag=True)
    for h in range(DSA_H):
        oT_ref[h * DSA_DH:(h + 1) * DSA_DH, :] = acc[h] / l[h]


def _dsa(qd, qi, w, k, v, ki, P, top_k):
    B, Tq, _ = qd.shape
    L = k.shape[1]
    Lp, Tp = _round_up(L, DSA_UNROLL * DSA_TK), _round_up(Tq, LANES)
    qT = lambda a: jnp.pad(a.swapaxes(1, 2), ((0, 0), (0, 0), (0, Tp - Tq)))
    kpad = lambda a: jnp.pad(a.astype(BF16), ((0, 0), (0, Lp - L), (0, 0)))
    k_pos = jnp.arange(Lp, dtype=jnp.int32)[:, None]
    feat = jnp.arange(LANES, dtype=jnp.int32)[None, :]
    posf = jnp.where(feat < 3, 1, jnp.where(feat < 6, k_pos // CHUNK, jnp.where(feat < 9, k_pos % CHUNK, 0)))
    qspec = lambda n: pl.BlockSpec((None, n, LANES), lambda b, i: (b, 0, i))
    kspec = lambda r, c: pl.BlockSpec((None, r, c), lambda b, i: (b, 0, 0), pipeline_mode=pl.Buffered(1))
    oT = pl.pallas_call(
        functools.partial(_dsa_kernel, P=P, L=L, top_k=top_k),
        grid=(B, Tp // LANES),
        in_specs=[qspec(DSA_W), qspec(IDX_Q), qspec(IDX_H), _const_spec((Lp, LANES)),
                  kspec(Lp, DSA_W), kspec(DSA_W, Lp), kspec(Lp, IDX_D)],
        out_specs=qspec(DSA_W),
        out_shape=jax.ShapeDtypeStruct((B, DSA_W, Tp), F32),
        scratch_shapes=[pltpu.VMEM((Lp, LANES), F32), pltpu.VMEM((DSA_UNROLL, DSA_H, DSA_TK, LANES), F32)],
        compiler_params=_params("parallel", "arbitrary"),
        name="dsa",
    )(qT(qd), qT(qi), qT(w), posf.astype(BF16), kpad(k), kpad(v).swapaxes(1, 2), kpad(ki))
    return oT[:, :, :Tq].swapaxes(1, 2)


def _merge_kernel(x_ref, og_ref, od_ref, ga_ref, gb_ref, gt_ref, wbg_ref, wbd_ref, wo_ref, lg_ref, lb_ref, o_ref,
                  *, alpha):
    x = x_ref[...]
    bb, tt, D = x.shape
    merged = (jax.nn.sigmoid(ga_ref[...]) * _dot(og_ref[...].astype(BF16), wbg_ref[...])
              + jax.nn.sigmoid(gb_ref[...]) * _dot(od_ref[...].astype(BF16), wbd_ref[...]))
    y = _dot(merged.astype(BF16), wo_ref[...]).reshape(bb, tt, D)
    o_ref[...] = _layer_norm(alpha * x + gt_ref[...] * y, lg_ref[...], lb_ref[...])


def _merge(x, og, od, ga, gb, mod, wbg, wbd, wo, lg, lb, alpha):
    B, T, D = x.shape
    bb, tt = _row_tiles(B, T)
    nj = T // tt
    xspec = pl.BlockSpec((bb, tt, D), lambda i, j: (i, j, 0))
    rspec = lambda n: pl.BlockSpec((bb * tt, n), lambda i, j: (i * nj + j, 0))
    return pl.pallas_call(
        functools.partial(_merge_kernel, alpha=alpha),
        grid=(B // bb, nj),
        in_specs=[xspec, rspec(GLA_V), rspec(DSA_W), rspec(D), rspec(D), _mod_spec(5, bb, D),
                  _const_spec(wbg.shape), _const_spec(wbd.shape), _const_spec(wo.shape),
                  _const_spec((1, D)), _const_spec((1, D))],
        out_specs=xspec,
        out_shape=jax.ShapeDtypeStruct((B, T, D), F32),
        compiler_params=_params("parallel", "parallel"),
        name="merge",
    )(x, og, od, ga, gb, mod, wbg, wbd, wo, lg, lb)


def _encoder_layer(x, mod, lp, past, alpha):
    B, T, D = x.shape
    x = _ffn(x, mod, (0, 1, 2), lp["wg"][0], lp["wu"][0], lp["wd"][0], lp["ln_g"][0:1], lp["ln_b"][0:1], alpha)
    gq, gk, gv, gg, la, dq, dk, dv, iq, ikw, ga, gb = _inproj(x, mod, lp["w_in"], lp["w_alpha"], lp["b_alpha"])
    b3 = lambda a: a.reshape(B, T, a.shape[-1])
    ik = b3(ikw)[:, :, :IDX_D]
    if past is None:
        s0 = jnp.zeros((B, GLA_QK, GLA_DV), F32)
        k_all, v_all, ki_all, P = b3(dk), b3(dv), ik, 0
        C = CHUNK
    else:
        ck, cv, cki, s_past = past
        P = ck.shape[1]
        s0 = s_past.reshape(B, GLA_QK, GLA_DV)
        k_all = jnp.concatenate([ck.reshape(B, P, DSA_W), b3(dk)], axis=1)
        v_all = jnp.concatenate([cv.reshape(B, P, DSA_W), b3(dv)], axis=1)
        ki_all = jnp.concatenate([cki, ik], axis=1)
        C = T
    og, s_new = _gla(b3(gq), b3(gk).swapaxes(1, 2), b3(la), b3(la).swapaxes(1, 2), b3(gv), b3(gg), s0,
                     lp["gla_g"], C)
    od = _dsa(b3(dq), b3(iq), b3(ikw)[:, :, IDX_D:IDX_D + IDX_H], k_all, v_all, ki_all, P,
              min(TOPK_MAX, (P + T) // 4))
    x = _merge(x, og.reshape(B * T, GLA_V), od.reshape(B * T, DSA_W), ga, gb, mod,
               lp["w_br_gla"], lp["w_br_dsa"], lp["w_out"], lp["ln_g"][1:2], lp["ln_b"][1:2], alpha)
    x = _ffn(x, mod, (6, 7, 8), lp["wg"][1], lp["wu"][1], lp["wd"][1], lp["ln_g"][2:3], lp["ln_b"][2:3], alpha)
    st = (dk.reshape(B, T, DSA_H, DSA_DH), dv.reshape(B, T, DSA_H, DSA_DH), ik,
          s_new.reshape(B, GLA_H, GLA_DK, GLA_DV))
    return x, st


def kernel(x_prompt, x_sample, cache_dsa_k, cache_dsa_v, cache_idx_k, state_gla, c_prompt, c_sample,
           w_ada, b_ada, ln_g, ln_b, ffn_w_gate, ffn_w_up, ffn_w_down, w_in, w_alpha, b_alpha,
           gla_norm_g, w_br_gla, w_br_dsa, w_out):
    depth = w_ada.shape[0]
    D = x_prompt.shape[-1]
    Bp, Bs = x_prompt.shape[0], x_sample.shape[0]
    alpha = (2 * depth) ** 0.25
    c_all = jnp.concatenate([c_prompt, c_sample], axis=0)
    xp, xs = x_prompt, x_sample
    outs = [[] for _ in range(8)]
    for l in range(depth):
        w_in_l, w_alpha_l = _prep_w_in(w_in[l], w_alpha[l], D)
        lp = dict(wg=ffn_w_gate[l].astype(BF16), wu=ffn_w_up[l].astype(BF16), wd=ffn_w_down[l].astype(BF16),
                  ln_g=ln_g[l], ln_b=ln_b[l], w_in=w_in_l, w_alpha=w_alpha_l, b_alpha=b_alpha[l][None, :],
                  gla_g=gla_norm_g[l][None, :], w_br_gla=w_br_gla[l].astype(BF16),
                  w_br_dsa=w_br_dsa[l].astype(BF16), w_out=w_out[l].astype(BF16))
        mod = _ada(c_all, w_ada[l].astype(BF16), b_ada[l][None, :])
        mod = mod.reshape(Bp + Bs, N_MOD, 1, D).swapaxes(0, 1)
        xp, st_p = _encoder_layer(xp, mod[:, :Bp], lp, None, alpha)
        past = (cache_dsa_k[l], cache_dsa_v[l], cache_idx_k[l], state_gla[l])
        xs, st_s = _encoder_layer(xs, mod[:, Bp:], lp, past, alpha)
        for i, a in enumerate(st_p + st_s):
            outs[i].append(a)
    return (xp, xs) + tuple(jnp.stack(o) for o in outs)
```

```python
import functools

import jax
import jax.numpy as jnp
from jax import lax
from jax.experimental import pallas as pl
from jax.experimental.pallas import tpu as pltpu

CHUNK = 64
Q_BLOCK = 128
GLA_H = 4
GLA_DK = 64
GLA_DV = 128
GLA_RANK = 16
GLA_TAU = 16.0
DSA_H = 8
DSA_DH = 64
IDX_H = 8
IDX_D = 64
TOPK_MAX = 256
N_MOD = 9
LN_EPS = 1e-5
GLA_QK = GLA_H * GLA_DK
GLA_V = GLA_H * GLA_DV
DSA_W = DSA_H * DSA_DH
IDX_Q = IDX_H * IDX_D

LANES = 128
VMEM_LIMIT_BYTES = 56 * 1024 * 1024
MASK_VALUE = -1e30
LOG2E = 1.4426950408889634
DSA_TK = 256
DSA_UNROLL = 2

F32 = jnp.float32
BF16 = jnp.bfloat16


def _dot(a, b):
    return jnp.dot(a, b, preferred_element_type=F32)


def _params(*sem):
    return pltpu.CompilerParams(dimension_semantics=sem, vmem_limit_bytes=VMEM_LIMIT_BYTES)


def _round_up(n, m):
    return (n + m - 1) // m * m


def _layer_norm(y, g, b):
    mu = jnp.mean(y, axis=-1, keepdims=True)
    d = y - mu
    var = jnp.mean(d * d, axis=-1, keepdims=True)
    return d * lax.rsqrt(var + LN_EPS) * g + b


def _silu(x):
    return x * jax.nn.sigmoid(x)


def _split3(a):
    hi = a.astype(BF16)
    r = a - hi.astype(F32)
    mid = r.astype(BF16)
    lo = (r - mid.astype(F32)).astype(BF16)
    return hi, mid, lo


def _row_tiles(B, T, rows=512):
    if T >= 256:
        return 1, (rows if T % rows == 0 else 256)
    bb = max(1, min(B, 256 // T))
    while B % bb:
        bb -= 1
    return bb, T


def _ada_kernel(c_ref, w_ref, b_ref, o_ref):
    o_ref[...] = _dot(c_ref[...].astype(BF16), w_ref[...]) + b_ref[...]


def _ada(c, w, b):
    Bc, D = c.shape
    N = w.shape[1]
    return pl.pallas_call(
        _ada_kernel,
        grid=(N // D,),
        in_specs=[pl.BlockSpec((Bc, D), lambda j: (0, 0)),
                  pl.BlockSpec((D, D), lambda j: (0, j)),
                  pl.BlockSpec((1, D), lambda j: (0, j))],
        out_specs=pl.BlockSpec((Bc, D), lambda j: (0, j)),
        out_shape=jax.ShapeDtypeStruct((Bc, N), F32),
        compiler_params=_params("arbitrary"),
        name="ada",
    )(c, w, b)


def _mod_spec(k, bb, D):
    return pl.BlockSpec((None, bb, 1, D), lambda i, j: (k, i, 0, 0))


def _const_spec(shape):
    return pl.BlockSpec(shape, lambda i, j: (0,) * len(shape), pipeline_mode=pl.Buffered(1))


def _ffn_kernel(x_ref, sh_ref, sc_ref, gt_ref, wg_ref, wu_ref, wd_ref, lg_ref, lb_ref, o_ref, *, alpha, fc):
    x = x_ref[...]
    bb, tt, D = x.shape
    h = (x * (1.0 + sc_ref[...]) + sh_ref[...]).reshape(bb * tt, D).astype(BF16)
    f = jnp.zeros((bb * tt, D), F32)
    for j in range(wg_ref.shape[1] // fc):
        g = _dot(h, wg_ref[:, j * fc:(j + 1) * fc])
        u = _dot(h, wu_ref[:, j * fc:(j + 1) * fc])
        f = f + _dot((_silu(g) * u).astype(BF16), wd_ref[j * fc:(j + 1) * fc, :])
    y = alpha * x + 0.5 * gt_ref[...] * f.reshape(bb, tt, D)
    o_ref[...] = _layer_norm(y, lg_ref[...], lb_ref[...])


def _ffn(x, mod, ks, wg, wu, wd, lg, lb, alpha):
    B, T, D = x.shape
    FF = wg.shape[1]
    bb, tt = _row_tiles(B, T)
    fc = next(c for c in (512, 256, 128) if FF % c == 0)
    xspec = pl.BlockSpec((bb, tt, D), lambda i, j: (i, j, 0))
    return pl.pallas_call(
        functools.partial(_ffn_kernel, alpha=alpha, fc=fc),
        grid=(B // bb, T // tt),
        in_specs=[xspec, _mod_spec(ks[0], bb, D), _mod_spec(ks[1], bb, D), _mod_spec(ks[2], bb, D),
                  _const_spec((D, FF)), _const_spec((D, FF)), _const_spec((FF, D)),
                  _const_spec((1, D)), _const_spec((1, D))],
        out_specs=xspec,
        out_shape=jax.ShapeDtypeStruct((B, T, D), F32),
        compiler_params=_params("parallel", "parallel"),
        name="ffn",
    )(x, mod, mod, mod, wg, wu, wd, lg, lb)


_IN_SEGS = (("gq", GLA_QK), ("gk", GLA_QK), ("gv", GLA_V), ("gg", GLA_V), ("glr", LANES),
            ("dq", DSA_W), ("dk", DSA_W), ("dv", DSA_W), ("iq", IDX_Q), ("ikw", LANES))


def _log_sigmoid(z):
    return jnp.minimum(z, 0.0) - jnp.log(1.0 + jnp.exp(-jnp.abs(z)))


def _in_kernel(x_ref, sh_ref, sc_ref, w_ref, wa_ref, ba_ref,
               gq_o, gk_o, gv_o, gg_o, la_o, dq_o, dk_o, dv_o, iq_o, ikw_o, ga_o, gb_o):
    x = x_ref[...]
    bb, tt, D = x.shape
    h = (x * (1.0 + sc_ref[...]) + sh_ref[...]).reshape(bb * tt, D).astype(BF16)
    outs = dict(gq=gq_o, gk=gk_o, gv=gv_o, gg=gg_o, dq=dq_o, dk=dk_o, dv=dv_o, iq=iq_o, ikw=ikw_o, ga=ga_o, gb=gb_o)
    off = 0
    for name, n in _IN_SEGS + (("ga", D), ("gb", D)):
        r = _dot(h, w_ref[:, off:off + n])
        off += n
        if name == "glr":
            z = _dot(r.astype(BF16), wa_ref[...]) + ba_ref[...]
            la_o[...] = _log_sigmoid(z) * (1.0 / GLA_TAU)
        elif name == "gq":
            gq_o[...] = r * (GLA_DK ** -0.5)
        else:
            outs[name][...] = r


def _prep_w_in(w_in, w_alpha, D):
    pts, o = {}, 0
    for name, n in (("gq", GLA_QK), ("gk", GLA_QK), ("gv", GLA_V), ("gg", GLA_V), ("glr", GLA_RANK),
                    ("dq", DSA_W), ("dk", DSA_W), ("dv", DSA_W), ("iq", IDX_Q), ("ik", IDX_D), ("iw", IDX_H),
                    ("ga", D), ("gb", D)):
        pts[name] = w_in[:, o:o + n]
        o += n
    zeros = lambda n: jnp.zeros((D, n), w_in.dtype)
    cols = [pts["gq"], pts["gk"], pts["gv"], pts["gg"], pts["glr"], zeros(LANES - GLA_RANK),
            pts["dq"], pts["dk"], pts["dv"], pts["iq"], pts["ik"], pts["iw"], zeros(LANES - IDX_D - IDX_H),
            pts["ga"], pts["gb"]]
    w = jnp.concatenate(cols, axis=1).astype(BF16)
    wa = jnp.concatenate([w_alpha, jnp.zeros((LANES - GLA_RANK, GLA_QK), w_alpha.dtype)], axis=0).astype(BF16)
    return w, wa


def _inproj(x, mod, w, wa, ba):
    B, T, D = x.shape
    M = B * T
    bb, tt = _row_tiles(B, T, rows=256)
    nj = T // tt
    widths = [GLA_QK, GLA_QK, GLA_V, GLA_V, GLA_QK, DSA_W, DSA_W, DSA_W, IDX_Q, LANES, D, D]
    ospec = lambda n: pl.BlockSpec((bb * tt, n), lambda i, j: (i * nj + j, 0))
    return pl.pallas_call(
        _in_kernel,
        grid=(B // bb, nj),
        in_specs=[pl.BlockSpec((bb, tt, D), lambda i, j: (i, j, 0)), _mod_spec(3, bb, D), _mod_spec(4, bb, D),
                  _const_spec(w.shape), _const_spec(wa.shape), _const_spec((1, GLA_QK))],
        out_specs=[ospec(n) for n in widths],
        out_shape=[jax.ShapeDtypeStruct((M, n), F32) for n in widths],
        compiler_params=_params("parallel", "parallel"),
        name="inproj",
    )(x, mod, mod, w, wa, ba)


def _gla_kernel(q_ref, kT_ref, la_ref, laT_ref, v_ref, gg_ref, s0_ref, ng_ref, o_ref, so_ref, s_ref, *, C, GW):
    j = pl.program_id(1)

    @pl.when(j == 0)
    def _():
        s_ref[...] = s0_ref[...]

    TB = q_ref.shape[0]
    G = GW // C
    r_i = lax.broadcasted_iota(jnp.int32, (GW, GW), 0)
    c_i = lax.broadcasted_iota(jnp.int32, (GW, GW), 1)
    same = (r_i // C) == (c_i // C)
    tri_lo = jnp.where(same & (c_i <= r_i), 1.0, 0.0).astype(BF16)
    tri_up = jnp.where(same & (r_i <= c_i), 1.0, 0.0).astype(BF16)
    lane = lax.broadcasted_iota(jnp.int32, (1, GW), 1)
    feat_head = lax.broadcasted_iota(jnp.int32, (1, GLA_QK), 1) // GLA_DK
    t_loc = lax.broadcasted_iota(jnp.int32, (C, GW), 0)
    s_lane = lax.broadcasted_iota(jnp.int32, (C, GW), 1)

    def group(g, carry):
        r0 = pl.multiple_of(g * GW, GW)
        q = q_ref[pl.ds(r0, GW), :]
        la = la_ref[pl.ds(r0, GW), :]
        v = v_ref[pl.ds(r0, GW), :].astype(BF16)
        gg = gg_ref[pl.ds(r0, GW), :]
        kT = kT_ref[:, pl.ds(r0, GW)]
        laT = laT_ref[:, pl.ds(r0, GW)]
        b = sum(_dot(tri_lo, p) for p in _split3(la))
        bT = sum(_dot(p, tri_up) for p in _split3(laT))
        qe = q * jnp.exp(b)
        keT = (kT * jnp.exp(-bT)).astype(BF16)
        for c in range(G):
            in_c = (lane // C) == c
            b_lastT = bT[:, c * C + C - 1:c * C + C]
            kdT = (kT * jnp.exp(jnp.where(in_c, b_lastT - bT, -jnp.inf))).astype(BF16)
            qc = qe[c * C:(c + 1) * C, :]
            qstack = jnp.concatenate([jnp.where(feat_head == h, qc, 0.0) for h in range(GLA_H)], axis=0).astype(BF16)
            s_old = s_ref[...]
            o_inter = _dot(qstack, s_old.astype(BF16))
            scores = _dot(qstack, keT)
            causal = ((s_lane // C) == c) & ((s_lane - c * C) <= t_loc)
            upd = _dot(kdT, v)
            upd = jnp.concatenate([upd[h * GLA_DK:(h + 1) * GLA_DK, h * GLA_DV:(h + 1) * GLA_DV]
                                   for h in range(GLA_H)], axis=0)
            s_ref[...] = jnp.exp(b_lastT) * s_old + upd
            for h in range(GLA_H):
                sc_h = jnp.where(causal, scores[h * C:(h + 1) * C, :], 0.0).astype(BF16)
                o_h = o_inter[h * C:(h + 1) * C, :] + _dot(sc_h, v[:, h * GLA_DV:(h + 1) * GLA_DV])
                o_h = o_h * lax.rsqrt(jnp.mean(o_h * o_h, axis=-1, keepdims=True) + LN_EPS) * ng_ref[...]
                o_h = o_h * _silu(gg[c * C:(c + 1) * C, h * GLA_DV:(h + 1) * GLA_DV])
                o_ref[pl.ds(r0 + c * C, C), h * GLA_DV:(h + 1) * GLA_DV] = o_h
        return carry

    lax.fori_loop(0, TB // GW, group, 0)
    so_ref[...] = s_ref[...]


def _gla(q, kT, la, laT, v, gg, s0, ng, C):
    B, T, _ = q.shape
    GW = min(LANES, T)
    TB = min(512, T)
    tok = lambda n: pl.BlockSpec((None, TB, n), lambda b, j: (b, j, 0))
    feat = pl.BlockSpec((None, GLA_QK, TB), lambda b, j: (b, 0, j))
    st = pl.BlockSpec((None, GLA_QK, GLA_DV), lambda b, j: (b, 0, 0))
    return pl.pallas_call(
        functools.partial(_gla_kernel, C=C, GW=GW),
        grid=(B, T // TB),
        in_specs=[tok(GLA_QK), feat, tok(GLA_QK), feat, tok(GLA_V), tok(GLA_V), st, _const_spec((1, GLA_DV))],
        out_specs=[tok(GLA_V), st],
        out_shape=[jax.ShapeDtypeStruct((B, T, GLA_V), F32), jax.ShapeDtypeStruct((B, GLA_QK, GLA_DV), F32)],
        scratch_shapes=[pltpu.VMEM((GLA_QK, GLA_DV), F32)],
        compiler_params=_params("parallel", "arbitrary"),
        name="gla",
    )(q, kT, la, laT, v, gg, s0, ng)


def _fold_rows(x, op):
    parts = [x[i:i + 8] for i in range(0, x.shape[0], 8)]
    while len(parts) > 1:
        parts = [op(parts[i], parts[i + 1]) for i in range(0, len(parts), 2)]
    return parts[0]


def _key_to_float(k):
    return lax.bitcast_convert_type(jnp.where(k >= 0, k, k ^ jnp.int32(0x7FFFFFFF)), F32)


def _dsa_kernel(qdT_ref, qiT_ref, wT_ref, posf_ref, k_ref, vT_ref, ki_ref, oT_ref, sc_ref, sc16_ref, s_ref,
                *, P, L, top_k):
    QB, TK = LANES, DSA_TK
    qb = pl.program_id(1)
    q0 = P + qb * QB
    n_cols = jnp.minimum(((q0 + QB - 1) // CHUNK + 1) * CHUNK, L)
    STEP = DSA_UNROLL * TK
    nt = (n_cols + STEP - 1) // STEP
    t_diag = q0 // STEP
    q_pos = q0 + lax.broadcasted_iota(jnp.int32, (1, QB), 1)
    q_chunk = q_pos // CHUNK
    all_sel = jnp.minimum((q_chunk + 1) * CHUNK, L) <= top_k
    row = lax.broadcasted_iota(jnp.int32, (TK, 1), 0)

    def tile_start(t):
        return pl.multiple_of(t * TK, TK)

    qiT = qiT_ref[...].astype(BF16)
    qi_pair = [jnp.concatenate([qiT[h * IDX_D:(h + 1) * IDX_D], qiT[(h + 1) * IDX_D:(h + 2) * IDX_D]], axis=1)
               for h in range(0, IDX_H, 2)]
    wT = wT_ref[...] * (IDX_H ** -0.5 * IDX_D ** -0.5)

    def steps(lo, hi, tile_fn, carry, **kw):
        def body(t, c):
            for u in range(DSA_UNROLL):
                c = tile_fn(t * DSA_UNROLL + u, c, slot=u, **kw)
            return c
        return lax.fori_loop(lo, hi, body, carry)

    def score_tile(t, carry, slot, diag):
        c0 = tile_start(t)
        ki = ki_ref[pl.ds(c0, TK), :]
        acc = jnp.zeros((TK, QB), F32)
        for hp in range(IDX_H // 2):
            lg = _dot(ki, qi_pair[hp])
            for e in range(2):
                acc = acc + jnp.maximum(lg[:, e * QB:(e + 1) * QB], 0.0) * wT[2 * hp + e:2 * hp + e + 1, :]
        if diag:
            k_pos = c0 + row
            acc = jnp.where(((k_pos // CHUNK) <= q_chunk) & (k_pos < L), acc, -jnp.inf)
        sc_ref[pl.ds(c0, TK), :] = acc
        sc16_ref[pl.ds(c0, TK), :] = acc.astype(BF16)
        return carry

    steps(0, t_diag, score_tile, 0, diag=False)
    steps(t_diag, nt, score_tile, 0, diag=True)

    def count(pred):
        def body(t, a):
            hit = jnp.where(pred(sc_ref[pl.ds(pl.multiple_of(t * STEP, STEP), STEP), :]), 1.0, 0.0)
            return a + _fold_rows(hit, jnp.add)
        return jnp.sum(lax.fori_loop(0, nt, body, jnp.zeros((8, QB), F32)), axis=0, keepdims=True)

    def count16(c16):
        def body(t, a):
            x = sc16_ref[pl.ds(pl.multiple_of(t * STEP, STEP), STEP), :]
            hit = jnp.where(x >= c16, jnp.ones((), BF16), jnp.zeros((), BF16))
            parts = [hit[i:i + 16] for i in range(0, STEP, 16)]
            while len(parts) > 1:
                parts = [parts[i] + parts[i + 1] for i in range(0, len(parts), 2)]
            f = parts[0].astype(F32)
            return a + f[:8] + f[8:]
        return jnp.sum(lax.fori_loop(0, nt, body, jnp.zeros((8, QB), F32)), axis=0, keepdims=True)

    def search16(i, key16):
        cand = key16 + lax.shift_left(jnp.int32(1), 15 - i)
        bits = jnp.where(cand >= 0, cand, cand ^ jnp.int32(0x7FFF)) & jnp.int32(0xFFFF)
        c16 = lax.bitcast_convert_type(lax.shift_left(bits, 16), F32).astype(BF16)
        return jnp.where(count16(c16) >= top_k, cand, key16)

    key16 = lax.fori_loop(0, 16, search16, jnp.full((1, QB), -2 ** 15, jnp.int32))
    key32 = jnp.where(key16 >= 0, lax.shift_left(key16, 16), lax.shift_left(key16, 16) | jnp.int32(0xFFFF))

    def search(i, tkey):
        cand = tkey + lax.shift_left(jnp.int32(1), 16 - i)
        cf = _key_to_float(cand)
        return jnp.where(count(lambda s: s >= cf) >= top_k, cand, tkey)

    tkey = lax.fori_loop(0, 17, search, key32 - jnp.int32(0x10000))
    thr = _key_to_float(tkey)
    n_ties = top_k - count(lambda s: s > thr)
    thr_sel = jnp.where(all_sel, -3e38, thr)
    n_ties = jnp.where(all_sel, 3e38, n_ties)

    r_i = lax.broadcasted_iota(jnp.int32, (TK, TK), 0)
    c_i = lax.broadcasted_iota(jnp.int32, (TK, TK), 1)
    tril = jnp.where(c_i <= r_i, 1.0, 0.0).astype(BF16)
    qdT = qdT_ref[...] * (DSA_DH ** -0.5 * LOG2E)
    zeros = jnp.zeros((DSA_DH, QB), F32)
    feat_row = lax.broadcasted_iota(jnp.int32, (LANES, 1), 0)
    q_pos_f = q_pos.astype(F32)
    rhs = []
    for hp in range(DSA_H // 2):
        h0 = 2 * hp * DSA_DH
        qbd = jnp.concatenate([jnp.concatenate([qdT[h0:h0 + DSA_DH], zeros], axis=1),
                               jnp.concatenate([zeros, qdT[h0 + DSA_DH:h0 + 2 * DSA_DH]], axis=1)], axis=0)
        pos = []
        for h in (2 * hp, 2 * hp + 1):
            slope = (2.0 ** -(h + 1)) * LOG2E
            parts = _split3(-slope * q_pos_f) + _split3(jnp.full((1, QB), slope * CHUNK, F32)) \
                + _split3(jnp.full((1, QB), slope, F32))
            m = jnp.zeros((LANES, QB), F32)
            for r, part in enumerate(parts):
                m = jnp.where(feat_row == r, part.astype(F32), m)
            pos.append(m)
        rhs.append(jnp.concatenate([qbd, jnp.concatenate(pos, axis=1)], axis=0).astype(BF16))

    def attend(t, carry, slot, diag):
        stage = s_ref.at[slot]
        m, l, acc, seen = carry
        c0 = tile_start(t)
        s_idx = sc_ref[pl.ds(c0, TK), :]
        eq = s_idx == thr
        rank = seen + _dot(tril, jnp.where(eq, 1.0, 0.0).astype(BF16))
        seen = rank[TK - 1:TK, :]
        s_eff = jnp.where(jnp.where(eq, rank, 0.0) > n_ties, -jnp.inf, s_idx)
        bias = jnp.where(s_eff >= thr_sel, 0.0, MASK_VALUE)
        kt = k_ref[pl.ds(c0, TK), :]
        vt = vT_ref[:, pl.ds(c0, TK)]
        pf = posf_ref[pl.ds(c0, TK), :]
        if diag:
            over = jnp.maximum(c0 + row - q_pos, 0).astype(F32)
        for hp in range(DSA_H // 2):
            sp = _dot(jnp.concatenate([kt[:, hp * LANES:(hp + 1) * LANES], pf], axis=1), rhs[hp])
            for e in range(2):
                h = 2 * hp + e
                s = sp[:, e * QB:(e + 1) * QB] + bias
                if diag:
                    s = s - (2.0 * (2.0 ** -(h + 1)) * LOG2E) * over
                stage[h] = s
        m_new = [jnp.maximum(m[h], jnp.max(_fold_rows(stage[h], jnp.maximum), axis=0, keepdims=True))
                 for h in range(DSA_H)]
        l_new, acc_new = [], []
        for h in range(DSA_H):
            p = jnp.exp2(stage[h] - m_new[h])
            ch = jnp.exp2(m[h] - m_new[h])
            l_new.append(l[h] * ch + jnp.sum(_fold_rows(p, jnp.add), axis=0, keepdims=True))
            acc_new.append(acc[h] * ch + _dot(vt[h * DSA_DH:(h + 1) * DSA_DH, :], p.astype(BF16)))
        return tuple(m_new), tuple(l_new), tuple(acc_new), seen

    carry = (tuple(jnp.full((1, QB), MASK_VALUE, F32) for _ in range(DSA_H)),
             tuple(jnp.zeros((1, QB), F32) for _ in range(DSA_H)),
             tuple(jnp.zeros((DSA_DH, QB), F32) for _ in range(DSA_H)),
             jnp.zeros((1, QB), F32))
    carry = steps(0, t_diag, attend, carry, diag=False)
    _, l, acc, _ = steps(t_diag, nt, attend, carry, diag=True)
    for h in range(DSA_H):
        oT_ref[h * DSA_DH:(h + 1) * DSA_DH, :] = acc[h] / l[h]


def _dsa(qd, qi, w, k, v, ki, P, top_k):
    B, Tq, _ = qd.shape
    L = k.shape[1]
    Lp, Tp = _round_up(L, DSA_UNROLL * DSA_TK), _round_up(Tq, LANES)
    qT = lambda a: jnp.pad(a.swapaxes(1, 2), ((0, 0), (0, 0), (0, Tp - Tq)))
    kpad = lambda a: jnp.pad(a.astype(BF16), ((0, 0), (0, Lp - L), (0, 0)))
    k_pos = jnp.arange(Lp, dtype=jnp.int32)[:, None]
    feat = jnp.arange(LANES, dtype=jnp.int32)[None, :]
    posf = jnp.where(feat < 3, 1, jnp.where(feat < 6, k_pos // CHUNK, jnp.where(feat < 9, k_pos % CHUNK, 0)))
    qspec = lambda n: pl.BlockSpec((None, n, LANES), lambda b, i: (b, 0, i))
    kspec = lambda r, c: pl.BlockSpec((None, r, c), lambda b, i: (b, 0, 0), pipeline_mode=pl.Buffered(1))
    oT = pl.pallas_call(
        functools.partial(_dsa_kernel, P=P, L=L, top_k=top_k),
        grid=(B, Tp // LANES),
        in_specs=[qspec(DSA_W), qspec(IDX_Q), qspec(IDX_H), _const_spec((Lp, LANES)),
                  kspec(Lp, DSA_W), kspec(DSA_W, Lp), kspec(Lp, IDX_D)],
        out_specs=qspec(DSA_W),
        out_shape=jax.ShapeDtypeStruct((B, DSA_W, Tp), F32),
        scratch_shapes=[pltpu.VMEM((Lp, LANES), F32), pltpu.VMEM((Lp, LANES), BF16),
                        pltpu.VMEM((DSA_UNROLL, DSA_H, DSA_TK, LANES), F32)],
        compiler_params=_params("parallel", "arbitrary"),
        name="dsa",
    )(qT(qd), qT(qi), qT(w), posf.astype(BF16), kpad(k), kpad(v).swapaxes(1, 2), kpad(ki))
    return oT[:, :, :Tq].swapaxes(1, 2)


def _merge_kernel(x_ref, og_ref, od_ref, ga_ref, gb_ref, gt_ref, wbg_ref, wbd_ref, wo_ref, lg_ref, lb_ref, o_ref,
                  *, alpha):
    x = x_ref[...]
    bb, tt, D = x.shape
    merged = (jax.nn.sigmoid(ga_ref[...]) * _dot(og_ref[...].astype(BF16), wbg_ref[...])
              + jax.nn.sigmoid(gb_ref[...]) * _dot(od_ref[...].astype(BF16), wbd_ref[...]))
    y = _dot(merged.astype(BF16), wo_ref[...]).reshape(bb, tt, D)
    o_ref[...] = _layer_norm(alpha * x + gt_ref[...] * y, lg_ref[...], lb_ref[...])


def _merge(x, og, od, ga, gb, mod, wbg, wbd, wo, lg, lb, alpha):
    B, T, D = x.shape
    bb, tt = _row_tiles(B, T)
    nj = T // tt
    xspec = pl.BlockSpec((bb, tt, D), lambda i, j: (i, j, 0))
    rspec = lambda n: pl.BlockSpec((bb * tt, n), lambda i, j: (i * nj + j, 0))
    return pl.pallas_call(
        functools.partial(_merge_kernel, alpha=alpha),
        grid=(B // bb, nj),
        in_specs=[xspec, rspec(GLA_V), rspec(DSA_W), rspec(D), rspec(D), _mod_spec(5, bb, D),
                  _const_spec(wbg.shape), _const_spec(wbd.shape), _const_spec(wo.shape),
                  _const_spec((1, D)), _const_spec((1, D))],
        out_specs=xspec,
        out_shape=jax.ShapeDtypeStruct((B, T, D), F32),
        compiler_params=_params("parallel", "parallel"),
        name="merge",
    )(x, og, od, ga, gb, mod, wbg, wbd, wo, lg, lb)


def _encoder_layer(x, mod, lp, past, alpha):
    B, T, D = x.shape
    x = _ffn(x, mod, (0, 1, 2), lp["wg"][0], lp["wu"][0], lp["wd"][0], lp["ln_g"][0:1], lp["ln_b"][0:1], alpha)
    gq, gk, gv, gg, la, dq, dk, dv, iq, ikw, ga, gb = _inproj(x, mod, lp["w_in"], lp["w_alpha"], lp["b_alpha"])
    b3 = lambda a: a.reshape(B, T, a.shape[-1])
    ik = b3(ikw)[:, :, :IDX_D]
    if past is None:
        s0 = jnp.zeros((B, GLA_QK, GLA_DV), F32)
        k_all, v_all, ki_all, P = b3(dk), b3(dv), ik, 0
        C = CHUNK
    else:
        ck, cv, cki, s_past = past
        P = ck.shape[1]
        s0 = s_past.reshape(B, GLA_QK, GLA_DV)
        k_all = jnp.concatenate([ck.reshape(B, P, DSA_W), b3(dk)], axis=1)
        v_all = jnp.concatenate([cv.reshape(B, P, DSA_W), b3(dv)], axis=1)
        ki_all = jnp.concatenate([cki, ik], axis=1)
        C = T
    og, s_new = _gla(b3(gq), b3(gk).swapaxes(1, 2), b3(la), b3(la).swapaxes(1, 2), b3(gv), b3(gg), s0,
                     lp["gla_g"], C)
    od = _dsa(b3(dq), b3(iq), b3(ikw)[:, :, IDX_D:IDX_D + IDX_H], k_all, v_all, ki_all, P,
              min(TOPK_MAX, (P + T) // 4))
    x = _merge(x, og.reshape(B * T, GLA_V), od.reshape(B * T, DSA_W), ga, gb, mod,
               lp["w_br_gla"], lp["w_br_dsa"], lp["w_out"], lp["ln_g"][1:2], lp["ln_b"][1:2], alpha)
    x = _ffn(x, mod, (6, 7, 8), lp["wg"][1], lp["wu"][1], lp["wd"][1], lp["ln_g"][2:3], lp["ln_b"][2:3], alpha)
    st = (dk.reshape(B, T, DSA_H, DSA_DH), dv.reshape(B, T, DSA_H, DSA_DH), ik,
          s_new.reshape(B, GLA_H, GLA_DK, GLA_DV))
    return x, st


def kernel(x_prompt, x_sample, cache_dsa_k, cache_dsa_v, cache_idx_k, state_gla, c_prompt, c_sample,
           w_ada, b_ada, ln_g, ln_b, ffn_w_gate, ffn_w_up, ffn_w_down, w_in, w_alpha, b_alpha,
           gla_norm_g, w_br_gla, w_br_dsa, w_out):
    depth = w_ada.shape[0]
    D = x_prompt.shape[-1]
    Bp, Bs = x_prompt.shape[0], x_sample.shape[0]
    alpha = (2 * depth) ** 0.25
    c_all = jnp.concatenate([c_prompt, c_sample], axis=0)
    xp, xs = x_prompt, x_sample
    outs = [[] for _ in range(8)]
    for l in range(depth):
        w_in_l, w_alpha_l = _prep_w_in(w_in[l], w_alpha[l], D)
        lp = dict(wg=ffn_w_gate[l].astype(BF16), wu=ffn_w_up[l].astype(BF16), wd=ffn_w_down[l].astype(BF16),
                  ln_g=ln_g[l], ln_b=ln_b[l], w_in=w_in_l, w_alpha=w_alpha_l, b_alpha=b_alpha[l][None, :],
                  gla_g=gla_norm_g[l][None, :], w_br_gla=w_br_gla[l].astype(BF16),
                  w_br_dsa=w_br_dsa[l].astype(BF16), w_out=w_out[l].astype(BF16))
        mod = _ada(c_all, w_ada[l].astype(BF16), b_ada[l][None, :])
        mod = mod.reshape(Bp + Bs, N_MOD, 1, D).swapaxes(0, 1)
        xp, st_p = _encoder_layer(xp, mod[:, :Bp], lp, None, alpha)
        past = (cache_dsa_k[l], cache_dsa_v[l], cache_idx_k[l], state_gla[l])
        xs, st_s = _encoder_layer(xs, mod[:, Bp:], lp, past, alpha)
        for i, a in enumerate(st_p + st_s):
            outs[i].append(a)
    return (xp, xs) + tuple(jnp.stack(o) for o in outs)
```

```python
import functools

import jax
import jax.numpy as jnp
from jax import lax
from jax.experimental import pallas as pl
from jax.experimental.pallas import tpu as pltpu

CHUNK = 64
Q_BLOCK = 128
GLA_H = 4
GLA_DK = 64
GLA_DV = 128
GLA_RANK = 16
GLA_TAU = 16.0
DSA_H = 8
DSA_DH = 64
IDX_H = 8
IDX_D = 64
TOPK_MAX = 256
N_MOD = 9
LN_EPS = 1e-5
GLA_QK = GLA_H * GLA_DK
GLA_V = GLA_H * GLA_DV
DSA_W = DSA_H * DSA_DH
IDX_Q = IDX_H * IDX_D

LANES = 128
VMEM_LIMIT_BYTES = 56 * 1024 * 1024
MASK_VALUE = -1e30
LOG2E = 1.4426950408889634
DSA_TK = 256
DSA_UNROLL = 2

F32 = jnp.float32
BF16 = jnp.bfloat16


def _dot(a, b):
    return jnp.dot(a, b, preferred_element_type=F32)


def _params(*sem):
    return pltpu.CompilerParams(dimension_semantics=sem, vmem_limit_bytes=VMEM_LIMIT_BYTES)


def _round_up(n, m):
    return (n + m - 1) // m * m


def _layer_norm(y, g, b):
    mu = jnp.mean(y, axis=-1, keepdims=True)
    d = y - mu
    var = jnp.mean(d * d, axis=-1, keepdims=True)
    return d * lax.rsqrt(var + LN_EPS) * g + b


def _silu(x):
    return x * jax.nn.sigmoid(x)


def _split3(a):
    hi = a.astype(BF16)
    r = a - hi.astype(F32)
    mid = r.astype(BF16)
    lo = (r - mid.astype(F32)).astype(BF16)
    return hi, mid, lo


def _row_tiles(B, T, rows=512):
    if T >= 256:
        return 1, (rows if T % rows == 0 else 256)
    bb = max(1, min(B, 256 // T))
    while B % bb:
        bb -= 1
    return bb, T


def _ada_kernel(c_ref, w_ref, b_ref, o_ref):
    o_ref[...] = _dot(c_ref[...].astype(BF16), w_ref[...]) + b_ref[...]


def _ada(c, w, b):
    Bc, D = c.shape
    N = w.shape[1]
    return pl.pallas_call(
        _ada_kernel,
        grid=(N // D,),
        in_specs=[pl.BlockSpec((Bc, D), lambda j: (0, 0)),
                  pl.BlockSpec((D, D), lambda j: (0, j)),
                  pl.BlockSpec((1, D), lambda j: (0, j))],
        out_specs=pl.BlockSpec((Bc, D), lambda j: (0, j)),
        out_shape=jax.ShapeDtypeStruct((Bc, N), F32),
        compiler_params=_params("arbitrary"),
        name="ada",
    )(c, w, b)


def _mod_spec(k, bb, D):
    return pl.BlockSpec((None, bb, 1, D), lambda i, j: (k, i, 0, 0))


def _const_spec(shape):
    return pl.BlockSpec(shape, lambda i, j: (0,) * len(shape), pipeline_mode=pl.Buffered(1))


def _ffn_kernel(x_ref, sh_ref, sc_ref, gt_ref, wg_ref, wu_ref, wd_ref, lg_ref, lb_ref, o_ref, *, alpha, fc):
    x = x_ref[...]
    bb, tt, D = x.shape
    h = (x * (1.0 + sc_ref[...]) + sh_ref[...]).reshape(bb * tt, D).astype(BF16)
    f = jnp.zeros((bb * tt, D), F32)
    for j in range(wg_ref.shape[1] // fc):
        g = _dot(h, wg_ref[:, j * fc:(j + 1) * fc])
        u = _dot(h, wu_ref[:, j * fc:(j + 1) * fc])
        f = f + _dot((_silu(g) * u).astype(BF16), wd_ref[j * fc:(j + 1) * fc, :])
    y = alpha * x + 0.5 * gt_ref[...] * f.reshape(bb, tt, D)
    o_ref[...] = _layer_norm(y, lg_ref[...], lb_ref[...])


def _ffn(x, mod, ks, wg, wu, wd, lg, lb, alpha):
    B, T, D = x.shape
    FF = wg.shape[1]
    bb, tt = _row_tiles(B, T)
    fc = next(c for c in (512, 256, 128) if FF % c == 0)
    xspec = pl.BlockSpec((bb, tt, D), lambda i, j: (i, j, 0))
    return pl.pallas_call(
        functools.partial(_ffn_kernel, alpha=alpha, fc=fc),
        grid=(B // bb, T // tt),
        in_specs=[xspec, _mod_spec(ks[0], bb, D), _mod_spec(ks[1], bb, D), _mod_spec(ks[2], bb, D),
                  _const_spec((D, FF)), _const_spec((D, FF)), _const_spec((FF, D)),
                  _const_spec((1, D)), _const_spec((1, D))],
        out_specs=xspec,
        out_shape=jax.ShapeDtypeStruct((B, T, D), F32),
        compiler_params=_params("parallel", "parallel"),
        name="ffn",
    )(x, mod, mod, mod, wg, wu, wd, lg, lb)


_IN_SEGS = (("gq", GLA_QK), ("gk", GLA_QK), ("gv", GLA_V), ("gg", GLA_V), ("glr", LANES),
            ("dq", DSA_W), ("dk", DSA_W), ("dv", DSA_W), ("iq", IDX_Q), ("ikw", LANES))


def _log_sigmoid(z):
    return jnp.minimum(z, 0.0) - jnp.log(1.0 + jnp.exp(-jnp.abs(z)))


def _in_kernel(x_ref, sh_ref, sc_ref, w_ref, wa_ref, ba_ref,
               gq_o, gk_o, gv_o, gg_o, la_o, dq_o, dk_o, dv_o, iq_o, ikw_o, ga_o, gb_o):
    x = x_ref[...]
    bb, tt, D = x.shape
    h = (x * (1.0 + sc_ref[...]) + sh_ref[...]).reshape(bb * tt, D).astype(BF16)
    outs = dict(gq=gq_o, gk=gk_o, gv=gv_o, gg=gg_o, dq=dq_o, dk=dk_o, dv=dv_o, iq=iq_o, ikw=ikw_o, ga=ga_o, gb=gb_o)
    off = 0
    for name, n in _IN_SEGS + (("ga", D), ("gb", D)):
        r = _dot(h, w_ref[:, off:off + n])
        off += n
        if name == "glr":
            z = _dot(r.astype(BF16), wa_ref[...]) + ba_ref[...]
            la_o[...] = _log_sigmoid(z) * (1.0 / GLA_TAU)
        elif name == "gq":
            gq_o[...] = r * (GLA_DK ** -0.5)
        else:
            outs[name][...] = r


def _prep_w_in(w_in, w_alpha, D):
    pts, o = {}, 0
    for name, n in (("gq", GLA_QK), ("gk", GLA_QK), ("gv", GLA_V), ("gg", GLA_V), ("glr", GLA_RANK),
                    ("dq", DSA_W), ("dk", DSA_W), ("dv", DSA_W), ("iq", IDX_Q), ("ik", IDX_D), ("iw", IDX_H),
                    ("ga", D), ("gb", D)):
        pts[name] = w_in[:, o:o + n]
        o += n
    zeros = lambda n: jnp.zeros((D, n), w_in.dtype)
    cols = [pts["gq"], pts["gk"], pts["gv"], pts["gg"], pts["glr"], zeros(LANES - GLA_RANK),
            pts["dq"], pts["dk"], pts["dv"], pts["iq"], pts["ik"], pts["iw"], zeros(LANES - IDX_D - IDX_H),
            pts["ga"], pts["gb"]]
    w = jnp.concatenate(cols, axis=1).astype(BF16)
    wa = jnp.concatenate([w_alpha, jnp.zeros((LANES - GLA_RANK, GLA_QK), w_alpha.dtype)], axis=0).astype(BF16)
    return w, wa


def _inproj(x, mod, w, wa, ba):
    B, T, D = x.shape
    M = B * T
    bb, tt = _row_tiles(B, T, rows=256)
    nj = T // tt
    widths = [GLA_QK, GLA_QK, GLA_V, GLA_V, GLA_QK, DSA_W, DSA_W, DSA_W, IDX_Q, LANES, D, D]
    ospec = lambda n: pl.BlockSpec((bb * tt, n), lambda i, j: (i * nj + j, 0))
    return pl.pallas_call(
        _in_kernel,
        grid=(B // bb, nj),
        in_specs=[pl.BlockSpec((bb, tt, D), lambda i, j: (i, j, 0)), _mod_spec(3, bb, D), _mod_spec(4, bb, D),
                  _const_spec(w.shape), _const_spec(wa.shape), _const_spec((1, GLA_QK))],
        out_specs=[ospec(n) for n in widths],
        out_shape=[jax.ShapeDtypeStruct((M, n), F32) for n in widths],
        compiler_params=_params("parallel", "parallel"),
        name="inproj",
    )(x, mod, mod, w, wa, ba)


def _gla_kernel(q_ref, kT_ref, la_ref, laT_ref, v_ref, gg_ref, s0_ref, ng_ref, o_ref, so_ref, s_ref, *, C, GW):
    j = pl.program_id(1)

    @pl.when(j == 0)
    def _():
        s_ref[...] = s0_ref[...]

    TB = q_ref.shape[0]
    G = GW // C
    r_i = lax.broadcasted_iota(jnp.int32, (GW, GW), 0)
    c_i = lax.broadcasted_iota(jnp.int32, (GW, GW), 1)
    same = (r_i // C) == (c_i // C)
    tri_lo = jnp.where(same & (c_i <= r_i), 1.0, 0.0).astype(BF16)
    tri_up = jnp.where(same & (r_i <= c_i), 1.0, 0.0).astype(BF16)
    lane = lax.broadcasted_iota(jnp.int32, (1, GW), 1)
    feat_head = lax.broadcasted_iota(jnp.int32, (1, GLA_QK), 1) // GLA_DK
    t_loc = lax.broadcasted_iota(jnp.int32, (C, GW), 0)
    s_lane = lax.broadcasted_iota(jnp.int32, (C, GW), 1)

    def group(g, carry):
        r0 = pl.multiple_of(g * GW, GW)
        q = q_ref[pl.ds(r0, GW), :]
        la = la_ref[pl.ds(r0, GW), :]
        v = v_ref[pl.ds(r0, GW), :].astype(BF16)
        gg = gg_ref[pl.ds(r0, GW), :]
        kT = kT_ref[:, pl.ds(r0, GW)]
        laT = laT_ref[:, pl.ds(r0, GW)]
        b = sum(_dot(tri_lo, p) for p in _split3(la))
        bT = sum(_dot(p, tri_up) for p in _split3(laT))
        qe = q * jnp.exp(b)
        keT = (kT * jnp.exp(-bT)).astype(BF16)
        for c in range(G):
            in_c = (lane // C) == c
            b_lastT = bT[:, c * C + C - 1:c * C + C]
            kdT = (kT * jnp.exp(jnp.where(in_c, b_lastT - bT, -jnp.inf))).astype(BF16)
            qc = qe[c * C:(c + 1) * C, :]
            qstack = jnp.concatenate([jnp.where(feat_head == h, qc, 0.0) for h in range(GLA_H)], axis=0).astype(BF16)
            s_old = s_ref[...]
            o_inter = _dot(qstack, s_old.astype(BF16))
            scores = _dot(qstack, keT)
            causal = ((s_lane // C) == c) & ((s_lane - c * C) <= t_loc)
            upd = _dot(kdT, v)
            upd = jnp.concatenate([upd[h * GLA_DK:(h + 1) * GLA_DK, h * GLA_DV:(h + 1) * GLA_DV]
                                   for h in range(GLA_H)], axis=0)
            s_ref[...] = jnp.exp(b_lastT) * s_old + upd
            for h in range(GLA_H):
                sc_h = jnp.where(causal, scores[h * C:(h + 1) * C, :], 0.0).astype(BF16)
                o_h = o_inter[h * C:(h + 1) * C, :] + _dot(sc_h, v[:, h * GLA_DV:(h + 1) * GLA_DV])
                o_h = o_h * lax.rsqrt(jnp.mean(o_h * o_h, axis=-1, keepdims=True) + LN_EPS) * ng_ref[...]
                o_h = o_h * _silu(gg[c * C:(c + 1) * C, h * GLA_DV:(h + 1) * GLA_DV])
                o_ref[pl.ds(r0 + c * C, C), h * GLA_DV:(h + 1) * GLA_DV] = o_h
        return carry

    lax.fori_loop(0, TB // GW, group, 0)
    so_ref[...] = s_ref[...]


def _gla(q, kT, la, laT, v, gg, s0, ng, C):
    B, T, _ = q.shape
    GW = min(LANES, T)
    TB = min(512, T)
    tok = lambda n: pl.BlockSpec((None, TB, n), lambda b, j: (b, j, 0))
    feat = pl.BlockSpec((None, GLA_QK, TB), lambda b, j: (b, 0, j))
    st = pl.BlockSpec((None, GLA_QK, GLA_DV), lambda b, j: (b, 0, 0))
    return pl.pallas_call(
        functools.partial(_gla_kernel, C=C, GW=GW),
        grid=(B, T // TB),
        in_specs=[tok(GLA_QK), feat, tok(GLA_QK), feat, tok(GLA_V), tok(GLA_V), st, _const_spec((1, GLA_DV))],
        out_specs=[tok(GLA_V), st],
        out_shape=[jax.ShapeDtypeStruct((B, T, GLA_V), F32), jax.ShapeDtypeStruct((B, GLA_QK, GLA_DV), F32)],
        scratch_shapes=[pltpu.VMEM((GLA_QK, GLA_DV), F32)],
        compiler_params=_params("parallel", "arbitrary"),
        name="gla",
    )(q, kT, la, laT, v, gg, s0, ng)


def _fold_rows(x, op):
    parts = [x[i:i + 8] for i in range(0, x.shape[0], 8)]
    while len(parts) > 1:
        parts = [op(parts[i], parts[i + 1]) for i in range(0, len(parts), 2)]
    return parts[0]


def _key_to_float(k):
    return lax.bitcast_convert_type(jnp.where(k >= 0, k, k ^ jnp.int32(0x7FFFFFFF)), F32)


def _dsa_kernel(qdT_ref, qiT_ref, wT_ref, posf_ref, k_ref, vT_ref, ki_ref, oT_ref, sc_ref, s_ref, *, P, L, top_k):
    QB, TK = LANES, DSA_TK
    qb = pl.program_id(1)
    q0 = P + qb * QB
    n_cols = jnp.minimum(((q0 + QB - 1) // CHUNK + 1) * CHUNK, L)
    STEP = DSA_UNROLL * TK
    nt = (n_cols + STEP - 1) // STEP
    t_diag = q0 // STEP
    q_pos = q0 + lax.broadcasted_iota(jnp.int32, (1, QB), 1)
    q_chunk = q_pos // CHUNK
    all_sel = jnp.minimum((q_chunk + 1) * CHUNK, L) <= top_k
    row = lax.broadcasted_iota(jnp.int32, (TK, 1), 0)

    def tile_start(t):
        return pl.multiple_of(t * TK, TK)

    qiT = qiT_ref[...].astype(BF16)
    qi_pair = [jnp.concatenate([qiT[h * IDX_D:(h + 1) * IDX_D], qiT[(h + 1) * IDX_D:(h + 2) * IDX_D]], axis=1)
               for h in range(0, IDX_H, 2)]
    wT = wT_ref[...] * (IDX_H ** -0.5 * IDX_D ** -0.5)

    def steps(lo, hi, tile_fn, carry, **kw):
        def body(t, c):
            for u in range(DSA_UNROLL):
                c = tile_fn(t * DSA_UNROLL + u, c, slot=u, **kw)
            return c
        return lax.fori_loop(lo, hi, body, carry)

    def score_tile(t, carry, slot, diag):
        c0 = tile_start(t)
        ki = ki_ref[pl.ds(c0, TK), :]
        acc = jnp.zeros((TK, QB), F32)
        for hp in range(IDX_H // 2):
            lg = _dot(ki, qi_pair[hp])
            for e in range(2):
                acc = acc + jnp.maximum(lg[:, e * QB:(e + 1) * QB], 0.0) * wT[2 * hp + e:2 * hp + e + 1, :]
        if diag:
            k_pos = c0 + row
            acc = jnp.where(((k_pos // CHUNK) <= q_chunk) & (k_pos < L), acc, -jnp.inf)
        sc_ref[pl.ds(c0, TK), :] = acc
        return carry

    steps(0, t_diag, score_tile, 0, diag=False)
    steps(t_diag, nt, score_tile, 0, diag=True)

    def count(pred):
        def body(t, accs):
            x = sc_ref[pl.ds(pl.multiple_of(t * STEP, STEP), STEP), :]
            accs = list(accs)
            for j in range(STEP // 8):
                a = accs[j % len(accs)]
                accs[j % len(accs)] = jnp.where(pred(x[8 * j:8 * j + 8]), a + 1.0, a)
            return tuple(accs)
        accs = lax.fori_loop(0, nt, body, tuple(jnp.zeros((8, QB), F32) for _ in range(8)))
        return jnp.sum(_fold_rows(jnp.concatenate(accs, axis=0), jnp.add), axis=0, keepdims=True)

    def search(i, tkey):
        cand = tkey + lax.shift_left(jnp.int32(1), 31 - i)
        cf = _key_to_float(cand)
        return jnp.where(count(lambda s: s >= cf) >= top_k, cand, tkey)

    tkey = lax.fori_loop(0, 32, search, jnp.full((1, QB), -2 ** 31, jnp.int32))
    thr = _key_to_float(tkey)
    n_ties = top_k - count(lambda s: s > thr)
    thr_sel = jnp.where(all_sel, -3e38, thr)
    n_ties = jnp.where(all_sel, 3e38, n_ties)

    r_i = lax.broadcasted_iota(jnp.int32, (TK, TK), 0)
    c_i = lax.broadcasted_iota(jnp.int32, (TK, TK), 1)
    tril = jnp.where(c_i <= r_i, 1.0, 0.0).astype(BF16)
    qdT = qdT_ref[...] * (DSA_DH ** -0.5 * LOG2E)
    zeros = jnp.zeros((DSA_DH, QB), F32)
    feat_row = lax.broadcasted_iota(jnp.int32, (LANES, 1), 0)
    q_pos_f = q_pos.astype(F32)
    rhs = []
    for hp in range(DSA_H // 2):
        h0 = 2 * hp * DSA_DH
        qbd = jnp.concatenate([jnp.concatenate([qdT[h0:h0 + DSA_DH], zeros], axis=1),
                               jnp.concatenate([zeros, qdT[h0 + DSA_DH:h0 + 2 * DSA_DH]], axis=1)], axis=0)
        pos = []
        for h in (2 * hp, 2 * hp + 1):
            slope = (2.0 ** -(h + 1)) * LOG2E
            parts = _split3(-slope * q_pos_f) + _split3(jnp.full((1, QB), slope * CHUNK, F32)) \
                + _split3(jnp.full((1, QB), slope, F32))
            m = jnp.zeros((LANES, QB), F32)
            for r, part in enumerate(parts):
                m = jnp.where(feat_row == r, part.astype(F32), m)
            pos.append(m)
        rhs.append(jnp.concatenate([qbd, jnp.concatenate(pos, axis=1)], axis=0).astype(BF16))

    def attend(t, carry, slot, diag):
        stage = s_ref.at[slot]
        m, l, acc, seen = carry
        c0 = tile_start(t)
        s_idx = sc_ref[pl.ds(c0, TK), :]
        eq = s_idx == thr
        rank = seen + _dot(tril, jnp.where(eq, 1.0, 0.0).astype(BF16))
        seen = rank[TK - 1:TK, :]
        s_eff = jnp.where(jnp.where(eq, rank, 0.0) > n_ties, -jnp.inf, s_idx)
        bias = jnp.where(s_eff >= thr_sel, 0.0, MASK_VALUE)
        kt = k_ref[pl.ds(c0, TK), :]
        vt = vT_ref[:, pl.ds(c0, TK)]
        pf = posf_ref[pl.ds(c0, TK), :]
        if diag:
            over = jnp.maximum(c0 + row - q_pos, 0).astype(F32)
        for hp in range(DSA_H // 2):
            sp = _dot(jnp.concatenate([kt[:, hp * LANES:(hp + 1) * LANES], pf], axis=1), rhs[hp])
            for e in range(2):
                h = 2 * hp + e
                s = sp[:, e * QB:(e + 1) * QB] + bias
                if diag:
                    s = s - (2.0 * (2.0 ** -(h + 1)) * LOG2E) * over
                stage[h] = s
        m_new = [jnp.maximum(m[h], jnp.max(_fold_rows(stage[h], jnp.maximum), axis=0, keepdims=True))
                 for h in range(DSA_H)]
        l_new, acc_new = [], []
        for h in range(DSA_H):
            p = jnp.exp2(stage[h] - m_new[h])
            ch = jnp.exp2(m[h] - m_new[h])
            l_new.append(l[h] * ch + jnp.sum(_fold_rows(p, jnp.add), axis=0, keepdims=True))
            acc_new.append(acc[h] * ch + _dot(vt[h * DSA_DH:(h + 1) * DSA_DH, :], p.astype(BF16)))
        return tuple(m_new), tuple(l_new), tuple(acc_new), seen

    carry = (tuple(jnp.full((1, QB), MASK_VALUE, F32) for _ in range(DSA_H)),
             tuple(jnp.zeros((1, QB), F32) for _ in range(DSA_H)),
             tuple(jnp.zeros((DSA_DH, QB), F32) for _ in range(DSA_H)),
             jnp.zeros((1, QB), F32))
    carry = steps(0, t_diag, attend, carry, diag=False)
    _, l, acc, _ = steps(t_diag, nt, attend, carry, diag=True)
    for h in range(DSA_H):
        oT_ref[h * DSA_DH:(h + 1) * DSA_DH, :] = acc[h] / l[h]


def _dsa(qd, qi, w, k, v, ki, P, top_k):
    B, Tq, _ = qd.shape
    L = k.shape[1]
    Lp, Tp = _round_up(L, DSA_UNROLL * DSA_TK), _round_up(Tq, LANES)
    qT = lambda a: jnp.pad(a.swapaxes(1, 2), ((0, 0), (0, 0), (0, Tp - Tq)))
    kpad = lambda a: jnp.pad(a.astype(BF16), ((0, 0), (0, Lp - L), (0, 0)))
    k_pos = jnp.arange(Lp, dtype=jnp.int32)[:, None]
    feat = jnp.arange(LANES, dtype=jnp.int32)[None, :]
    posf = jnp.where(feat < 3, 1, jnp.where(feat < 6, k_pos // CHUNK, jnp.where(feat < 9, k_pos % CHUNK, 0)))
    qspec = lambda n: pl.BlockSpec((None, n, LANES), lambda b, i: (b, 0, i))
    kspec = lambda r, c: pl.BlockSpec((None, r, c), lambda b, i: (b, 0, 0), pipeline_mode=pl.Buffered(1))
    oT = pl.pallas_call(
        functools.partial(_dsa_kernel, P=P, L=L, top_k=top_k),
        grid=(B, Tp // LANES),
        in_specs=[qspec(DSA_W), qspec(IDX_Q), qspec(IDX_H), _const_spec((Lp, LANES)),
                  kspec(Lp, DSA_W), kspec(DSA_W, Lp), kspec(Lp, IDX_D)],
        out_specs=qspec(DSA_W),
        out_shape=jax.ShapeDtypeStruct((B, DSA_W, Tp), F32),
        scratch_shapes=[pltpu.VMEM((Lp, LANES), F32), pltpu.VMEM((DSA_UNROLL, DSA_H, DSA_TK, LANES), F32)],
        compiler_params=_params("parallel", "arbitrary"),
        name="dsa",
    )(qT(qd), qT(qi), qT(w), posf.astype(BF16), kpad(k), kpad(v).swapaxes(1, 2), kpad(ki))
    return oT[:, :, :Tq].swapaxes(1, 2)


def _merge_kernel(x_ref, og_ref, od_ref, ga_ref, gb_ref, gt_ref, wbg_ref, wbd_ref, wo_ref, lg_ref, lb_ref, o_ref,
                  *, alpha):
    x = x_ref[...]
    bb, tt, D = x.shape
    merged = (jax.nn.sigmoid(ga_ref[...]) * _dot(og_ref[...].astype(BF16), wbg_ref[...])
              + jax.nn.sigmoid(gb_ref[...]) * _dot(od_ref[...].astype(BF16), wbd_ref[...]))
    y = _dot(merged.astype(BF16), wo_ref[...]).reshape(bb, tt, D)
    o_ref[...] = _layer_norm(alpha * x + gt_ref[...] * y, lg_ref[...], lb_ref[...])


def _merge(x, og, od, ga, gb, mod, wbg, wbd, wo, lg, lb, alpha):
    B, T, D = x.shape
    bb, tt = _row_tiles(B, T)
    nj = T // tt
    xspec = pl.BlockSpec((bb, tt, D), lambda i, j: (i, j, 0))
    rspec = lambda n: pl.BlockSpec((bb * tt, n), lambda i, j: (i * nj + j, 0))
    return pl.pallas_call(
        functools.partial(_merge_kernel, alpha=alpha),
        grid=(B // bb, nj),
        in_specs=[xspec, rspec(GLA_V), rspec(DSA_W), rspec(D), rspec(D), _mod_spec(5, bb, D),
                  _const_spec(wbg.shape), _const_spec(wbd.shape), _const_spec(wo.shape),
                  _const_spec((1, D)), _const_spec((1, D))],
        out_specs=xspec,
        out_shape=jax.ShapeDtypeStruct((B, T, D), F32),
        compiler_params=_params("parallel", "parallel"),
        name="merge",
    )(x, og, od, ga, gb, mod, wbg, wbd, wo, lg, lb)


def _encoder_layer(x, mod, lp, past, alpha):
    B, T, D = x.shape
    x = _ffn(x, mod, (0, 1, 2), lp["wg"][0], lp["wu"][0], lp["wd"][0], lp["ln_g"][0:1], lp["ln_b"][0:1], alpha)
    gq, gk, gv, gg, la, dq, dk, dv, iq, ikw, ga, gb = _inproj(x, mod, lp["w_in"], lp["w_alpha"], lp["b_alpha"])
    b3 = lambda a: a.reshape(B, T, a.shape[-1])
    ik = b3(ikw)[:, :, :IDX_D]
    if past is None:
        s0 = jnp.zeros((B, GLA_QK, GLA_DV), F32)
        k_all, v_all, ki_all, P = b3(dk), b3(dv), ik, 0
        C = CHUNK
    else:
        ck, cv, cki, s_past = past
        P = ck.shape[1]
        s0 = s_past.reshape(B, GLA_QK, GLA_DV)
        k_all = jnp.concatenate([ck.reshape(B, P, DSA_W), b3(dk)], axis=1)
        v_all = jnp.concatenate([cv.reshape(B, P, DSA_W), b3(dv)], axis=1)
        ki_all = jnp.concatenate([cki, ik], axis=1)
        C = T
    og, s_new = _gla(b3(gq), b3(gk).swapaxes(1, 2), b3(la), b3(la).swapaxes(1, 2), b3(gv), b3(gg), s0,
                     lp["gla_g"], C)
    od = _dsa(b3(dq), b3(iq), b3(ikw)[:, :, IDX_D:IDX_D + IDX_H], k_all, v_all, ki_all, P,
              min(TOPK_MAX, (P + T) // 4))
    x = _merge(x, og.reshape(B * T, GLA_V), od.reshape(B * T, DSA_W), ga, gb, mod,
               lp["w_br_gla"], lp["w_br_dsa"], lp["w_out"], lp["ln_g"][1:2], lp["ln_b"][1:2], alpha)
    x = _ffn(x, mod, (6, 7, 8), lp["wg"][1], lp["wu"][1], lp["wd"][1], lp["ln_g"][2:3], lp["ln_b"][2:3], alpha)
    st = (dk.reshape(B, T, DSA_H, DSA_DH), dv.reshape(B, T, DSA_H, DSA_DH), ik,
          s_new.reshape(B, GLA_H, GLA_DK, GLA_DV))
    return x, st


def kernel(x_prompt, x_sample, cache_dsa_k, cache_dsa_v, cache_idx_k, state_gla, c_prompt, c_sample,
           w_ada, b_ada, ln_g, ln_b, ffn_w_gate, ffn_w_up, ffn_w_down, w_in, w_alpha, b_alpha,
           gla_norm_g, w_br_gla, w_br_dsa, w_out):
    depth = w_ada.shape[0]
    D = x_prompt.shape[-1]
    Bp, Bs = x_prompt.shape[0], x_sample.shape[0]
    alpha = (2 * depth) ** 0.25
    c_all = jnp.concatenate([c_prompt, c_sample], axis=0)
    xp, xs = x_prompt, x_sample
    outs = [[] for _ in range(8)]
    for l in range(depth):
        w_in_l, w_alpha_l = _prep_w_in(w_in[l], w_alpha[l], D)
        lp = dict(wg=ffn_w_gate[l].astype(BF16), wu=ffn_w_up[l].astype(BF16), wd=ffn_w_down[l].astype(BF16),
                  ln_g=ln_g[l], ln_b=ln_b[l], w_in=w_in_l, w_alpha=w_alpha_l, b_alpha=b_alpha[l][None, :],
                  gla_g=gla_norm_g[l][None, :], w_br_gla=w_br_gla[l].astype(BF16),
                  w_br_dsa=w_br_dsa[l].astype(BF16), w_out=w_out[l].astype(BF16))
        mod = _ada(c_all, w_ada[l].astype(BF16), b_ada[l][None, :])
        mod = mod.reshape(Bp + Bs, N_MOD, 1, D).swapaxes(0, 1)
        xp, st_p = _encoder_layer(xp, mod[:, :Bp], lp, None, alpha)
        past = (cache_dsa_k[l], cache_dsa_v[l], cache_idx_k[l], state_gla[l])
        xs, st_s = _encoder_layer(xs, mod[:, Bp:], lp, past, alpha)
        for i, a in enumerate(st_p + st_s):
            outs[i].append(a)
    return (xp, xs) + tuple(jnp.stack(o) for o in outs)
```

```python
import functools

import jax
import jax.numpy as jnp
from jax import lax
from jax.experimental import pallas as pl
from jax.experimental.pallas import tpu as pltpu

CHUNK = 64
Q_BLOCK = 128
GLA_H = 4
GLA_DK = 64
GLA_DV = 128
GLA_RANK = 16
GLA_TAU = 16.0
DSA_H = 8
DSA_DH = 64
IDX_H = 8
IDX_D = 64
TOPK_MAX = 256
N_MOD = 9
LN_EPS = 1e-5
GLA_QK = GLA_H * GLA_DK
GLA_V = GLA_H * GLA_DV
DSA_W = DSA_H * DSA_DH
IDX_Q = IDX_H * IDX_D

LANES = 128
VMEM_LIMIT_BYTES = 56 * 1024 * 1024
MASK_VALUE = -1e30
LOG2E = 1.4426950408889634
DSA_TK = 256
DSA_UNROLL = 2

F32 = jnp.float32
BF16 = jnp.bfloat16


def _dot(a, b):
    return jnp.dot(a, b, preferred_element_type=F32)


def _params(*sem):
    return pltpu.CompilerParams(dimension_semantics=sem, vmem_limit_bytes=VMEM_LIMIT_BYTES)


def _round_up(n, m):
    return (n + m - 1) // m * m


def _layer_norm(y, g, b):
    mu = jnp.mean(y, axis=-1, keepdims=True)
    d = y - mu
    var = jnp.mean(d * d, axis=-1, keepdims=True)
    return d * lax.rsqrt(var + LN_EPS) * g + b


def _silu(x):
    return x * jax.nn.sigmoid(x)


def _split3(a):
    hi = a.astype(BF16)
    r = a - hi.astype(F32)
    mid = r.astype(BF16)
    lo = (r - mid.astype(F32)).astype(BF16)
    return hi, mid, lo


def _row_tiles(B, T, rows=512):
    if T >= 256:
        return 1, (rows if T % rows == 0 else 256)
    bb = max(1, min(B, 256 // T))
    while B % bb:
        bb -= 1
    return bb, T


def _ada_kernel(c_ref, w_ref, b_ref, o_ref):
    o_ref[...] = _dot(c_ref[...].astype(BF16), w_ref[...]) + b_ref[...]


def _ada(c, w, b):
    Bc, D = c.shape
    N = w.shape[1]
    return pl.pallas_call(
        _ada_kernel,
        grid=(N // D,),
        in_specs=[pl.BlockSpec((Bc, D), lambda j: (0, 0)),
                  pl.BlockSpec((D, D), lambda j: (0, j)),
                  pl.BlockSpec((1, D), lambda j: (0, j))],
        out_specs=pl.BlockSpec((Bc, D), lambda j: (0, j)),
        out_shape=jax.ShapeDtypeStruct((Bc, N), F32),
        compiler_params=_params("arbitrary"),
        name="ada",
    )(c, w, b)


def _mod_spec(k, bb, D):
    return pl.BlockSpec((None, bb, 1, D), lambda i, j: (k, i, 0, 0))


def _const_spec(shape):
    return pl.BlockSpec(shape, lambda i, j: (0,) * len(shape), pipeline_mode=pl.Buffered(1))


def _ffn_kernel(x_ref, sh_ref, sc_ref, gt_ref, wg_ref, wu_ref, wd_ref, lg_ref, lb_ref, o_ref, *, alpha, fc):
    x = x_ref[...]
    bb, tt, D = x.shape
    h = (x * (1.0 + sc_ref[...]) + sh_ref[...]).reshape(bb * tt, D).astype(BF16)
    f = jnp.zeros((bb * tt, D), F32)
    for j in range(wg_ref.shape[1] // fc):
        g = _dot(h, wg_ref[:, j * fc:(j + 1) * fc])
        u = _dot(h, wu_ref[:, j * fc:(j + 1) * fc])
        f = f + _dot((_silu(g) * u).astype(BF16), wd_ref[j * fc:(j + 1) * fc, :])
    y = alpha * x + 0.5 * gt_ref[...] * f.reshape(bb, tt, D)
    o_ref[...] = _layer_norm(y, lg_ref[...], lb_ref[...])


def _ffn(x, mod, ks, wg, wu, wd, lg, lb, alpha):
    B, T, D = x.shape
    FF = wg.shape[1]
    bb, tt = _row_tiles(B, T)
    fc = next(c for c in (512, 256, 128) if FF % c == 0)
    xspec = pl.BlockSpec((bb, tt, D), lambda i, j: (i, j, 0))
    return pl.pallas_call(
        functools.partial(_ffn_kernel, alpha=alpha, fc=fc),
        grid=(B // bb, T // tt),
        in_specs=[xspec, _mod_spec(ks[0], bb, D), _mod_spec(ks[1], bb, D), _mod_spec(ks[2], bb, D),
                  _const_spec((D, FF)), _const_spec((D, FF)), _const_spec((FF, D)),
                  _const_spec((1, D)), _const_spec((1, D))],
        out_specs=xspec,
        out_shape=jax.ShapeDtypeStruct((B, T, D), F32),
        compiler_params=_params("parallel", "parallel"),
        name="ffn",
    )(x, mod, mod, mod, wg, wu, wd, lg, lb)


_IN_SEGS = (("gq", GLA_QK), ("gk", GLA_QK), ("gv", GLA_V), ("gg", GLA_V), ("glr", LANES),
            ("dq", DSA_W), ("dk", DSA_W), ("dv", DSA_W), ("iq", IDX_Q), ("ikw", LANES))


def _log_sigmoid(z):
    return jnp.minimum(z, 0.0) - jnp.log(1.0 + jnp.exp(-jnp.abs(z)))


def _in_kernel(x_ref, sh_ref, sc_ref, w_ref, wa_ref, ba_ref,
               gq_o, gk_o, gv_o, gg_o, la_o, dq_o, dk_o, dv_o, iq_o, ikw_o, ga_o, gb_o):
    x = x_ref[...]
    bb, tt, D = x.shape
    h = (x * (1.0 + sc_ref[...]) + sh_ref[...]).reshape(bb * tt, D).astype(BF16)
    outs = dict(gq=gq_o, gk=gk_o, gv=gv_o, gg=gg_o, dq=dq_o, dk=dk_o, dv=dv_o, iq=iq_o, ikw=ikw_o, ga=ga_o, gb=gb_o)
    off = 0
    for name, n in _IN_SEGS + (("ga", D), ("gb", D)):
        r = _dot(h, w_ref[:, off:off + n])
        off += n
        if name == "glr":
            z = _dot(r.astype(BF16), wa_ref[...]) + ba_ref[...]
            la_o[...] = _log_sigmoid(z) * (1.0 / GLA_TAU)
        elif name == "gq":
            gq_o[...] = r * (GLA_DK ** -0.5)
        else:
            outs[name][...] = r


def _prep_w_in(w_in, w_alpha, D):
    pts, o = {}, 0
    for name, n in (("gq", GLA_QK), ("gk", GLA_QK), ("gv", GLA_V), ("gg", GLA_V), ("glr", GLA_RANK),
                    ("dq", DSA_W), ("dk", DSA_W), ("dv", DSA_W), ("iq", IDX_Q), ("ik", IDX_D), ("iw", IDX_H),
                    ("ga", D), ("gb", D)):
        pts[name] = w_in[:, o:o + n]
        o += n
    zeros = lambda n: jnp.zeros((D, n), w_in.dtype)
    cols = [pts["gq"], pts["gk"], pts["gv"], pts["gg"], pts["glr"], zeros(LANES - GLA_RANK),
            pts["dq"], pts["dk"], pts["dv"], pts["iq"], pts["ik"], pts["iw"], zeros(LANES - IDX_D - IDX_H),
            pts["ga"], pts["gb"]]
    w = jnp.concatenate(cols, axis=1).astype(BF16)
    wa = jnp.concatenate([w_alpha, jnp.zeros((LANES - GLA_RANK, GLA_QK), w_alpha.dtype)], axis=0).astype(BF16)
    return w, wa


def _inproj(x, mod, w, wa, ba):
    B, T, D = x.shape
    M = B * T
    bb, tt = _row_tiles(B, T, rows=256)
    nj = T // tt
    widths = [GLA_QK, GLA_QK, GLA_V, GLA_V, GLA_QK, DSA_W, DSA_W, DSA_W, IDX_Q, LANES, D, D]
    ospec = lambda n: pl.BlockSpec((bb * tt, n), lambda i, j: (i * nj + j, 0))
    return pl.pallas_call(
        _in_kernel,
        grid=(B // bb, nj),
        in_specs=[pl.BlockSpec((bb, tt, D), lambda i, j: (i, j, 0)), _mod_spec(3, bb, D), _mod_spec(4, bb, D),
                  _const_spec(w.shape), _const_spec(wa.shape), _const_spec((1, GLA_QK))],
        out_specs=[ospec(n) for n in widths],
        out_shape=[jax.ShapeDtypeStruct((M, n), F32) for n in widths],
        compiler_params=_params("parallel", "parallel"),
        name="inproj",
    )(x, mod, mod, w, wa, ba)


def _gla_kernel(q_ref, kT_ref, la_ref, laT_ref, v_ref, gg_ref, s0_ref, ng_ref, o_ref, so_ref, s_ref, *, C, GW):
    j = pl.program_id(1)

    @pl.when(j == 0)
    def _():
        s_ref[...] = s0_ref[...]

    TB = q_ref.shape[0]
    G = GW // C
    r_i = lax.broadcasted_iota(jnp.int32, (GW, GW), 0)
    c_i = lax.broadcasted_iota(jnp.int32, (GW, GW), 1)
    same = (r_i // C) == (c_i // C)
    tri_lo = jnp.where(same & (c_i <= r_i), 1.0, 0.0).astype(BF16)
    tri_up = jnp.where(same & (r_i <= c_i), 1.0, 0.0).astype(BF16)
    lane = lax.broadcasted_iota(jnp.int32, (1, GW), 1)
    feat_head = lax.broadcasted_iota(jnp.int32, (1, GLA_QK), 1) // GLA_DK
    t_loc = lax.broadcasted_iota(jnp.int32, (C, GW), 0)
    s_lane = lax.broadcasted_iota(jnp.int32, (C, GW), 1)

    def group(g, carry):
        r0 = pl.multiple_of(g * GW, GW)
        q = q_ref[pl.ds(r0, GW), :]
        la = la_ref[pl.ds(r0, GW), :]
        v = v_ref[pl.ds(r0, GW), :].astype(BF16)
        gg = gg_ref[pl.ds(r0, GW), :]
        kT = kT_ref[:, pl.ds(r0, GW)]
        laT = laT_ref[:, pl.ds(r0, GW)]
        b = sum(_dot(tri_lo, p) for p in _split3(la))
        bT = sum(_dot(p, tri_up) for p in _split3(laT))
        qe = q * jnp.exp(b)
        keT = (kT * jnp.exp(-bT)).astype(BF16)
        for c in range(G):
            in_c = (lane // C) == c
            b_lastT = bT[:, c * C + C - 1:c * C + C]
            kdT = (kT * jnp.exp(jnp.where(in_c, b_lastT - bT, -jnp.inf))).astype(BF16)
            qc = qe[c * C:(c + 1) * C, :]
            qstack = jnp.concatenate([jnp.where(feat_head == h, qc, 0.0) for h in range(GLA_H)], axis=0).astype(BF16)
            s_old = s_ref[...]
            o_inter = _dot(qstack, s_old.astype(BF16))
            scores = _dot(qstack, keT)
            causal = ((s_lane // C) == c) & ((s_lane - c * C) <= t_loc)
            upd = _dot(kdT, v)
            upd = jnp.concatenate([upd[h * GLA_DK:(h + 1) * GLA_DK, h * GLA_DV:(h + 1) * GLA_DV]
                                   for h in range(GLA_H)], axis=0)
            s_ref[...] = jnp.exp(b_lastT) * s_old + upd
            for h in range(GLA_H):
                sc_h = jnp.where(causal, scores[h * C:(h + 1) * C, :], 0.0).astype(BF16)
                o_h = o_inter[h * C:(h + 1) * C, :] + _dot(sc_h, v[:, h * GLA_DV:(h + 1) * GLA_DV])
                o_h = o_h * lax.rsqrt(jnp.mean(o_h * o_h, axis=-1, keepdims=True) + LN_EPS) * ng_ref[...]
                o_h = o_h * _silu(gg[c * C:(c + 1) * C, h * GLA_DV:(h + 1) * GLA_DV])
                o_ref[pl.ds(r0 + c * C, C), h * GLA_DV:(h + 1) * GLA_DV] = o_h
        return carry

    lax.fori_loop(0, TB // GW, group, 0)
    so_ref[...] = s_ref[...]


def _gla(q, kT, la, laT, v, gg, s0, ng, C):
    B, T, _ = q.shape
    GW = min(LANES, T)
    TB = min(512, T)
    tok = lambda n: pl.BlockSpec((None, TB, n), lambda b, j: (b, j, 0))
    feat = pl.BlockSpec((None, GLA_QK, TB), lambda b, j: (b, 0, j))
    st = pl.BlockSpec((None, GLA_QK, GLA_DV), lambda b, j: (b, 0, 0))
    return pl.pallas_call(
        functools.partial(_gla_kernel, C=C, GW=GW),
        grid=(B, T // TB),
        in_specs=[tok(GLA_QK), feat, tok(GLA_QK), feat, tok(GLA_V), tok(GLA_V), st, _const_spec((1, GLA_DV))],
        out_specs=[tok(GLA_V), st],
        out_shape=[jax.ShapeDtypeStruct((B, T, GLA_V), F32), jax.ShapeDtypeStruct((B, GLA_QK, GLA_DV), F32)],
        scratch_shapes=[pltpu.VMEM((GLA_QK, GLA_DV), F32)],
        compiler_params=_params("parallel", "arbitrary"),
        name="gla",
    )(q, kT, la, laT, v, gg, s0, ng)


def _fold_rows(x, op):
    parts = [x[i:i + 8] for i in range(0, x.shape[0], 8)]
    while len(parts) > 1:
        parts = [op(parts[i], parts[i + 1]) for i in range(0, len(parts), 2)]
    return parts[0]


def _key_to_float(k):
    return lax.bitcast_convert_type(jnp.where(k >= 0, k, k ^ jnp.int32(0x7FFFFFFF)), F32)


def _dsa_kernel(qdT_ref, qiT_ref, wT_ref, posf_ref, k_ref, vT_ref, ki_ref, oT_ref, sc_ref, s_ref, rhs_ref,
                tril_ref, *, P, L, top_k):
    QB, TK = LANES, DSA_TK
    qb = pl.program_id(1)
    q0 = P + qb * QB
    n_cols = jnp.minimum(((q0 + QB - 1) // CHUNK + 1) * CHUNK, L)
    STEP = DSA_UNROLL * TK
    nt = (n_cols + STEP - 1) // STEP
    t_diag = q0 // STEP
    q_pos = q0 + lax.broadcasted_iota(jnp.int32, (1, QB), 1)
    q_chunk = q_pos // CHUNK
    all_sel = jnp.minimum((q_chunk + 1) * CHUNK, L) <= top_k
    row = lax.broadcasted_iota(jnp.int32, (TK, 1), 0)

    def tile_start(t):
        return pl.multiple_of(t * TK, TK)

    qiT = qiT_ref[...].astype(BF16)
    qi_pair = [jnp.concatenate([qiT[h * IDX_D:(h + 1) * IDX_D], qiT[(h + 1) * IDX_D:(h + 2) * IDX_D]], axis=1)
               for h in range(0, IDX_H, 2)]
    wT = wT_ref[...] * (IDX_H ** -0.5 * IDX_D ** -0.5)

    def steps(lo, hi, tile_fn, carry, **kw):
        def body(t, c):
            for u in range(DSA_UNROLL):
                c = tile_fn(t * DSA_UNROLL + u, c, slot=u, **kw)
            return c
        return lax.fori_loop(lo, hi, body, carry)

    def score_tile(t, carry, slot, diag):
        c0 = tile_start(t)
        ki = ki_ref[pl.ds(c0, TK), :]
        acc = jnp.zeros((TK, QB), F32)
        for hp in range(IDX_H // 2):
            lg = _dot(ki, qi_pair[hp])
            for e in range(2):
                acc = acc + jnp.maximum(lg[:, e * QB:(e + 1) * QB], 0.0) * wT[2 * hp + e:2 * hp + e + 1, :]
        if diag:
            k_pos = c0 + row
            acc = jnp.where(((k_pos // CHUNK) <= q_chunk) & (k_pos < L), acc, -jnp.inf)
        sc_ref[pl.ds(c0, TK), :] = acc
        return carry

    steps(0, t_diag, score_tile, 0, diag=False)
    steps(t_diag, nt, score_tile, 0, diag=True)

    def count(pred):
        def body(t, accs):
            x = sc_ref[pl.ds(pl.multiple_of(t * STEP, STEP), STEP), :]
            accs = list(accs)
            for j in range(STEP // 8):
                a = accs[j % len(accs)]
                accs[j % len(accs)] = jnp.where(pred(x[8 * j:8 * j + 8]), a + 1.0, a)
            return tuple(accs)
        accs = lax.fori_loop(0, nt, body, tuple(jnp.zeros((8, QB), F32) for _ in range(8)))
        return jnp.sum(_fold_rows(jnp.concatenate(accs, axis=0), jnp.add), axis=0, keepdims=True)

    def search(i, tkey):
        cand = tkey + lax.shift_left(jnp.int32(1), 31 - i)
        cf = _key_to_float(cand)
        return jnp.where(count(lambda s: s >= cf) >= top_k, cand, tkey)

    tkey = lax.fori_loop(0, 32, search, jnp.full((1, QB), -2 ** 31, jnp.int32))
    thr = _key_to_float(tkey)
    n_ties = top_k - count(lambda s: s > thr)
    thr_sel = jnp.where(all_sel, -3e38, thr)
    n_ties = jnp.where(all_sel, 3e38, n_ties)

    r_i = lax.broadcasted_iota(jnp.int32, (TK, TK), 0)
    c_i = lax.broadcasted_iota(jnp.int32, (TK, TK), 1)
    tril_ref[...] = jnp.where(c_i <= r_i, 1.0, 0.0).astype(BF16)
    qdT = qdT_ref[...] * (DSA_DH ** -0.5 * LOG2E)
    zeros = jnp.zeros((DSA_DH, QB), F32)
    feat_row = lax.broadcasted_iota(jnp.int32, (LANES, 1), 0)
    q_pos_f = q_pos.astype(F32)
    for hp in range(DSA_H // 2):
        h0 = 2 * hp * DSA_DH
        qbd = jnp.concatenate([jnp.concatenate([qdT[h0:h0 + DSA_DH], zeros], axis=1),
                               jnp.concatenate([zeros, qdT[h0 + DSA_DH:h0 + 2 * DSA_DH]], axis=1)], axis=0)
        pos = []
        for h in (2 * hp, 2 * hp + 1):
            slope = (2.0 ** -(h + 1)) * LOG2E
            parts = _split3(-slope * q_pos_f) + _split3(jnp.full((1, QB), slope * CHUNK, F32)) \
                + _split3(jnp.full((1, QB), slope, F32))
            m = jnp.zeros((LANES, QB), F32)
            for r, part in enumerate(parts):
                m = jnp.where(feat_row == r, part.astype(F32), m)
            pos.append(m)
        rhs_ref[hp] = jnp.concatenate([qbd, jnp.concatenate(pos, axis=1)], axis=0).astype(BF16)

    def attend(t, carry, slot, diag):
        stage = s_ref.at[slot]
        m, l, acc, seen = carry
        c0 = tile_start(t)
        s_idx = sc_ref[pl.ds(c0, TK), :]
        eq = s_idx == thr
        rank = seen + _dot(tril_ref[...], jnp.where(eq, 1.0, 0.0).astype(BF16))
        seen = rank[TK - 1:TK, :]
        s_eff = jnp.where(jnp.where(eq, rank, 0.0) > n_ties, -jnp.inf, s_idx)
        bias = jnp.where(s_eff >= thr_sel, 0.0, MASK_VALUE)
        kt = k_ref[pl.ds(c0, TK), :]
        vt = vT_ref[:, pl.ds(c0, TK)]
        pf = posf_ref[pl.ds(c0, TK), :]
        if diag:
            over = jnp.maximum(c0 + row - q_pos, 0).astype(F32)
        for hp in range(DSA_H // 2):
            sp = _dot(jnp.concatenate([kt[:, hp * LANES:(hp + 1) * LANES], pf], axis=1), rhs_ref[hp])
            for e in range(2):
                h = 2 * hp + e
                s = sp[:, e * QB:(e + 1) * QB] + bias
                if diag:
                    s = s - (2.0 * (2.0 ** -(h + 1)) * LOG2E) * over
                stage[h] = s
        m_new = [jnp.maximum(m[h], jnp.max(_fold_rows(stage[h], jnp.maximum), axis=0, keepdims=True))
                 for h in range(DSA_H)]
        l_new, acc_new = [], []
        for h in range(DSA_H):
            p = jnp.exp2(stage[h] - m_new[h])
            ch = jnp.exp2(m[h] - m_new[h])
            l_new.append(l[h] * ch + jnp.sum(_fold_rows(p, jnp.add), axis=0, keepdims=True))
            acc_new.append(acc[h] * ch + _dot(vt[h * DSA_DH:(h + 1) * DSA_DH, :], p.astype(BF16)))
        return tuple(m_new), tuple(l_new), tuple(acc_new), seen

    carry = (tuple(jnp.full((1, QB), MASK_VALUE, F32) for _ in range(DSA_H)),
             tuple(jnp.zeros((1, QB), F32) for _ in range(DSA_H)),
             tuple(jnp.zeros((DSA_DH, QB), F32) for _ in range(DSA_H)),
             jnp.zeros((1, QB), F32))
    carry = steps(0, t_diag, attend, carry, diag=False)
    _, l, acc, _ = steps(t_diag, nt, attend, carry, diag=True)
    for h in range(DSA_H):
        oT_ref[h * DSA_DH:(h + 1) * DSA_DH, :] = acc[h] / l[h]


def _dsa(qd, qi, w, k, v, ki, P, top_k):
    B, Tq, _ = qd.shape
    L = k.shape[1]
    Lp, Tp = _round_up(L, DSA_UNROLL * DSA_TK), _round_up(Tq, LANES)
    qT = lambda a: jnp.pad(a.swapaxes(1, 2), ((0, 0), (0, 0), (0, Tp - Tq)))
    kpad = lambda a: jnp.pad(a.astype(BF16), ((0, 0), (0, Lp - L), (0, 0)))
    k_pos = jnp.arange(Lp, dtype=jnp.int32)[:, None]
    feat = jnp.arange(LANES, dtype=jnp.int32)[None, :]
    posf = jnp.where(feat < 3, 1, jnp.where(feat < 6, k_pos // CHUNK, jnp.where(feat < 9, k_pos % CHUNK, 0)))
    qspec = lambda n: pl.BlockSpec((None, n, LANES), lambda b, i: (b, 0, i))
    kspec = lambda r, c: pl.BlockSpec((None, r, c), lambda b, i: (b, 0, 0), pipeline_mode=pl.Buffered(1))
    oT = pl.pallas_call(
        functools.partial(_dsa_kernel, P=P, L=L, top_k=top_k),
        grid=(B, Tp // LANES),
        in_specs=[qspec(DSA_W), qspec(IDX_Q), qspec(IDX_H), _const_spec((Lp, LANES)),
                  kspec(Lp, DSA_W), kspec(DSA_W, Lp), kspec(Lp, IDX_D)],
        out_specs=qspec(DSA_W),
        out_shape=jax.ShapeDtypeStruct((B, DSA_W, Tp), F32),
        scratch_shapes=[pltpu.VMEM((Lp, LANES), F32), pltpu.VMEM((DSA_UNROLL, DSA_H, DSA_TK, LANES), F32),
                        pltpu.VMEM((DSA_H // 2, 2 * LANES, 2 * LANES), BF16), pltpu.VMEM((DSA_TK, DSA_TK), BF16)],
        compiler_params=_params("parallel", "arbitrary"),
        name="dsa",
    )(qT(qd), qT(qi), qT(w), posf.astype(BF16), kpad(k), kpad(v).swapaxes(1, 2), kpad(ki))
    return oT[:, :, :Tq].swapaxes(1, 2)


def _merge_kernel(x_ref, og_ref, od_ref, ga_ref, gb_ref, gt_ref, wbg_ref, wbd_ref, wo_ref, lg_ref, lb_ref, o_ref,
                  *, alpha):
    x = x_ref[...]
    bb, tt, D = x.shape
    merged = (jax.nn.sigmoid(ga_ref[...]) * _dot(og_ref[...].astype(BF16), wbg_ref[...])
              + jax.nn.sigmoid(gb_ref[...]) * _dot(od_ref[...].astype(BF16), wbd_ref[...]))
    y = _dot(merged.astype(BF16), wo_ref[...]).reshape(bb, tt, D)
    o_ref[...] = _layer_norm(alpha * x + gt_ref[...] * y, lg_ref[...], lb_ref[...])


def _merge(x, og, od, ga, gb, mod, wbg, wbd, wo, lg, lb, alpha):
    B, T, D = x.shape
    bb, tt = _row_tiles(B, T)
    nj = T // tt
    xspec = pl.BlockSpec((bb, tt, D), lambda i, j: (i, j, 0))
    rspec = lambda n: pl.BlockSpec((bb * tt, n), lambda i, j: (i * nj + j, 0))
    return pl.pallas_call(
        functools.partial(_merge_kernel, alpha=alpha),
        grid=(B // bb, nj),
        in_specs=[xspec, rspec(GLA_V), rspec(DSA_W), rspec(D), rspec(D), _mod_spec(5, bb, D),
                  _const_spec(wbg.shape), _const_spec(wbd.shape), _const_spec(wo.shape),
                  _const_spec((1, D)), _const_spec((1, D))],
        out_specs=xspec,
        out_shape=jax.ShapeDtypeStruct((B, T, D), F32),
        compiler_params=_params("parallel", "parallel"),
        name="merge",
    )(x, og, od, ga, gb, mod, wbg, wbd, wo, lg, lb)


def _encoder_layer(x, mod, lp, past, alpha):
    B, T, D = x.shape
    x = _ffn(x, mod, (0, 1, 2), lp["wg"][0], lp["wu"][0], lp["wd"][0], lp["ln_g"][0:1], lp["ln_b"][0:1], alpha)
    gq, gk, gv, gg, la, dq, dk, dv, iq, ikw, ga, gb = _inproj(x, mod, lp["w_in"], lp["w_alpha"], lp["b_alpha"])
    b3 = lambda a: a.reshape(B, T, a.shape[-1])
    ik = b3(ikw)[:, :, :IDX_D]
    if past is None:
        s0 = jnp.zeros((B, GLA_QK, GLA_DV), F32)
        k_all, v_all, ki_all, P = b3(dk), b3(dv), ik, 0
        C = CHUNK
    else:
        ck, cv, cki, s_past = past
        P = ck.shape[1]
        s0 = s_past.reshape(B, GLA_QK, GLA_DV)
        k_all = jnp.concatenate([ck.reshape(B, P, DSA_W), b3(dk)], axis=1)
        v_all = jnp.concatenate([cv.reshape(B, P, DSA_W), b3(dv)], axis=1)
        ki_all = jnp.concatenate([cki, ik], axis=1)
        C = T
    og, s_new = _gla(b3(gq), b3(gk).swapaxes(1, 2), b3(la), b3(la).swapaxes(1, 2), b3(gv), b3(gg), s0,
                     lp["gla_g"], C)
    od = _dsa(b3(dq), b3(iq), b3(ikw)[:, :, IDX_D:IDX_D + IDX_H], k_all, v_all, ki_all, P,
              min(TOPK_MAX, (P + T) // 4))
    x = _merge(x, og.reshape(B * T, GLA_V), od.reshape(B * T, DSA_W), ga, gb, mod,
               lp["w_br_gla"], lp["w_br_dsa"], lp["w_out"], lp["ln_g"][1:2], lp["ln_b"][1:2], alpha)
    x = _ffn(x, mod, (6, 7, 8), lp["wg"][1], lp["wu"][1], lp["wd"][1], lp["ln_g"][2:3], lp["ln_b"][2:3], alpha)
    st = (dk.reshape(B, T, DSA_H, DSA_DH), dv.reshape(B, T, DSA_H, DSA_DH), ik,
          s_new.reshape(B, GLA_H, GLA_DK, GLA_DV))
    return x, st


def kernel(x_prompt, x_sample, cache_dsa_k, cache_dsa_v, cache_idx_k, state_gla, c_prompt, c_sample,
           w_ada, b_ada, ln_g, ln_b, ffn_w_gate, ffn_w_up, ffn_w_down, w_in, w_alpha, b_alpha,
           gla_norm_g, w_br_gla, w_br_dsa, w_out):
    depth = w_ada.shape[0]
    D = x_prompt.shape[-1]
    Bp, Bs = x_prompt.shape[0], x_sample.shape[0]
    alpha = (2 * depth) ** 0.25
    c_all = jnp.concatenate([c_prompt, c_sample], axis=0)
    xp, xs = x_prompt, x_sample
    outs = [[] for _ in range(8)]
    for l in range(depth):
        w_in_l, w_alpha_l = _prep_w_in(w_in[l], w_alpha[l], D)
        lp = dict(wg=ffn_w_gate[l].astype(BF16), wu=ffn_w_up[l].astype(BF16), wd=ffn_w_down[l].astype(BF16),
                  ln_g=ln_g[l], ln_b=ln_b[l], w_in=w_in_l, w_alpha=w_alpha_l, b_alpha=b_alpha[l][None, :],
                  gla_g=gla_norm_g[l][None, :], w_br_gla=w_br_gla[l].astype(BF16),
                  w_br_dsa=w_br_dsa[l].astype(BF16), w_out=w_out[l].astype(BF16))
        mod = _ada(c_all, w_ada[l].astype(BF16), b_ada[l][None, :])
        mod = mod.reshape(Bp + Bs, N_MOD, 1, D).swapaxes(0, 1)
        xp, st_p = _encoder_layer(xp, mod[:, :Bp], lp, None, alpha)
        past = (cache_dsa_k[l], cache_dsa_v[l], cache_idx_k[l], state_gla[l])
        xs, st_s = _encoder_layer(xs, mod[:, Bp:], lp, past, alpha)
        for i, a in enumerate(st_p + st_s):
            outs[i].append(a)
    return (xp, xs) + tuple(jnp.stack(o) for o in outs)
```
